```python
import math
import jax, jax.numpy as jnp
from jax import lax
import numpy as np

D_MODEL = 1024
BATCH = 8
SEQ = 2048
DEPTH = 4

D_MIX = D_MODEL
D_MLSTM = D_MIX // 2
D_SB = D_MIX - D_MLSTM
H_MLSTM = 4
DH_MLSTM = D_MLSTM // H_MLSTM
H_SB = 8
DH_SB = D_SB // H_SB
CONV_K = 4
MLSTM_CHUNK = 64
SB_BLOCK = 128
P_IN = 4 * D_MLSTM + 2 * H_MLSTM + 3 * D_SB
D_FF_DENSE = 2752
N_EXPERTS = 8
TOP_K = 2
D_FF_EXPERT = 3584
N_DENSE_LAYERS = (DEPTH + 1) // 2
N_MOE_LAYERS = DEPTH // 2
EPS = 1e-6
M_INIT = -1e30

kernel_name = 'hybrid_mlstm_stickbreak_moe'


def _rmsnorm(x, g):
    xf = x.astype(jnp.float32)
    y = xf * lax.rsqrt(jnp.mean(xf * xf, axis=-1, keepdims=True) + EPS)
    return (y * g.astype(jnp.float32)).astype(x.dtype)


def _headwise_rmsnorm(x, g, n_heads):
    B, S, D = x.shape
    xf = x.astype(jnp.float32).reshape(B, S, n_heads, D // n_heads)
    y = xf * lax.rsqrt(jnp.mean(xf * xf, axis=-1, keepdims=True) + EPS)
    return y.reshape(B, S, D) * g.astype(jnp.float32)


def _to_heads(x, n_heads):
    B, S, D = x.shape
    return x.reshape(B, S, n_heads, D // n_heads).transpose(0, 2, 1, 3)


def _from_heads(x):
    B, H, S, dh = x.shape
    return x.transpose(0, 2, 1, 3).reshape(B, S, H * dh)


def _split_columns(proj):
    widths = (D_MLSTM, D_MLSTM, D_MLSTM, D_MLSTM, H_MLSTM, H_MLSTM, D_SB, D_SB, D_SB)
    out = []
    off = 0
    for w in widths:
        out.append(proj[..., off:off + w])
        off += w
    return out


def _causal_conv_silu(x, w, b):
    S = x.shape[1]
    xp = jnp.pad(x, ((0, 0), (CONV_K - 1, 0), (0, 0)))
    y = b
    for tap in range(CONV_K):
        y = y + xp[:, tap:tap + S, :] * w[tap]
    return jax.nn.silu(y)


def _mlstm_chunkwise(q, k, v, i_pre, f_pre):
    B, H, S, DH = q.shape
    L = MLSTM_CHUNK
    NC = S // L
    q = q.astype(jnp.float32)
    k = k.astype(jnp.float32) * (DH ** -0.5)
    v = v.astype(jnp.float32)
    log_i = i_pre.astype(jnp.float32)
    log_f = jax.nn.log_sigmoid(f_pre.astype(jnp.float32))

    def to_chunks(a):
        return jnp.moveaxis(a.reshape(B, H, NC, L, *a.shape[3:]), 2, 0)

    qc, kc, vc = to_chunks(q), to_chunks(k), to_chunks(v)
    ic, fc = to_chunks(log_i), to_chunks(log_f)
    causal = jnp.tril(jnp.ones((L, L), dtype=bool))

    def step(carry, inp):
        C, n, m = carry
        q_, k_, v_, i_, lf = inp
        b = jnp.cumsum(lf, axis=-1)
        d = b[..., :, None] - b[..., None, :] + i_[..., None, :]
        d = jnp.where(causal, d, -jnp.inf)
        inter = b + m[..., None]
        m_t = jnp.maximum(inter, jnp.max(d, axis=-1))
        w = jnp.exp(d - m_t[..., None])
        s = jnp.einsum('bhtd,bhsd->bhts', q_, k_) * w
        a_inter = jnp.exp(inter - m_t)
        num = a_inter[..., None] * jnp.einsum('bhtd,bhde->bhte', q_, C) + jnp.einsum('bhts,bhse->bhte', s, v_)
        den = a_inter * jnp.einsum('bhtd,bhd->bht', q_, n) + jnp.sum(s, axis=-1)
        h = num / jnp.maximum(jnp.abs(den), jnp.exp(-m_t))[..., None]
        b_last = b[..., -1]
        g = b_last[..., None] - b + i_
        m_new = jnp.maximum(b_last + m, jnp.max(g, axis=-1))
        decay = jnp.exp(b_last + m - m_new)
        wk = jnp.exp(g - m_new[..., None])[..., None] * k_
        C_new = decay[..., None, None] * C + jnp.einsum('bhsd,bhse->bhde', wk, v_)
        n_new = decay[..., None] * n + jnp.sum(wk, axis=2)
        return (C_new, n_new, m_new), h

    init = (jnp.zeros((B, H, DH, DH), jnp.float32),
            jnp.zeros((B, H, DH), jnp.float32),
            jnp.full((B, H), M_INIT, jnp.float32))
    _, hc = lax.scan(step, init, (qc, kc, vc, ic, fc))
    return jnp.moveaxis(hc, 0, 2).reshape(B, H, S, DH)


def _stick_breaking_attention(q, k, v):
    B, H, S, DH = q.shape
    scale = DH ** -0.5
    outs = []
    for blk in range(S // SB_BLOCK):
        t0 = blk * SB_BLOCK
        t1 = t0 + SB_BLOCK
        q_blk = q[:, :, t0:t1].astype(jnp.float32)
        k_blk = k[:, :, :t1].astype(jnp.float32)
        v_blk = v[:, :, :t1].astype(jnp.float32)
        z = jnp.einsum('bhtd,bhsd->bhts', q_blk, k_blk) * scale
        t_idx = t0 + jnp.arange(SB_BLOCK)[:, None]
        s_idx = jnp.arange(t1)[None, :]
        strict = s_idx < t_idx
        log_beta = jax.nn.log_sigmoid(z)
        log_rest = jnp.where(strict, jax.nn.log_sigmoid(-z), 0.0)
        rev = lax.cumsum(log_rest, axis=3, reverse=True)
        after = jnp.pad(rev[..., 1:], ((0, 0), (0, 0), (0, 0), (0, 1)))
        a = jnp.exp(jnp.where(strict, log_beta + after, -jnp.inf))
        outs.append(jnp.einsum('bhts,bhse->bhte', a, v_blk))
    return jnp.concatenate(outs, axis=2)


def _hybrid_mixer(h, w_in, b_igate, b_fgate, conv_w, conv_b, g_mlstm, g_sb, w_out):
    proj = h @ w_in
    q_m, k_m, v_m, o_m, i_pre, f_pre, q_s, k_s, v_s = _split_columns(proj)
    qk = _causal_conv_silu(jnp.concatenate([q_m, k_m], axis=-1), conv_w, conv_b)
    q_m, k_m = qk[..., :D_MLSTM], qk[..., D_MLSTM:]
    i_h = (i_pre + b_igate).transpose(0, 2, 1)
    f_h = (f_pre + b_fgate).transpose(0, 2, 1)
    h_m = _mlstm_chunkwise(_to_heads(q_m, H_MLSTM), _to_heads(k_m, H_MLSTM),
                           _to_heads(v_m, H_MLSTM), i_h, f_h)
    h_m = jax.nn.sigmoid(o_m.astype(jnp.float32)) * _headwise_rmsnorm(_from_heads(h_m), g_mlstm, H_MLSTM)
    h_s = _stick_breaking_attention(_to_heads(q_s, H_SB), _to_heads(k_s, H_SB), _to_heads(v_s, H_SB))
    h_s = _headwise_rmsnorm(_from_heads(h_s), g_sb, H_SB)
    mixed = jnp.concatenate([h_m, h_s], axis=-1).astype(h.dtype)
    return (mixed @ w_out).astype(h.dtype)


def _swiglu(h, w_gate, w_up, w_down):
    return (jax.nn.silu(h @ w_gate) * (h @ w_up)) @ w_down


def _moe_swiglu(h, w_router, w_gate, w_up, w_down):
    logits = (h @ w_router).astype(jnp.float32)
    top_val, top_idx = lax.top_k(logits, TOP_K)
    top_w = jax.nn.softmax(top_val, axis=-1)
    combine = jnp.sum(jax.nn.one_hot(top_idx, N_EXPERTS, dtype=jnp.float32) * top_w[..., None], axis=-2)
    combine = combine.astype(h.dtype)
    out = jnp.zeros_like(h)
    for e in range(N_EXPERTS):
        out = out + combine[..., e:e + 1] * _swiglu(h, w_gate[e], w_up[e], w_down[e])
    return out


def setup_inputs(seed: int = 0) -> dict:
    key = jax.random.key(seed)
    ks = jax.random.split(key, 20)
    f32 = jnp.float32
    nrm = lambda k, shape, fan_in: jax.random.normal(k, shape, f32) * (fan_in ** -0.5)
    return {
        'x': jax.random.normal(ks[0], (BATCH, SEQ, D_MODEL), f32),
        'norm_mix_g': 1.0 + 0.02 * jax.random.normal(ks[1], (DEPTH, D_MODEL), f32),
        'w_in': nrm(ks[2], (DEPTH, D_MODEL, P_IN), D_MODEL),
        'b_igate': 0.1 * jax.random.normal(ks[3], (DEPTH, H_MLSTM), f32),
        'b_fgate': 3.0 + 3.0 * jax.random.uniform(ks[4], (DEPTH, H_MLSTM), f32),
        'conv_w': nrm(ks[5], (DEPTH, CONV_K, 2 * D_MLSTM), CONV_K),
        'conv_b': 0.02 * jax.random.normal(ks[6], (DEPTH, 2 * D_MLSTM), f32),
        'g_mlstm': 1.0 + 0.02 * jax.random.normal(ks[7], (DEPTH, D_MLSTM), f32),
        'g_sb': 1.0 + 0.02 * jax.random.normal(ks[8], (DEPTH, D_SB), f32),
        'w_out': nrm(ks[9], (DEPTH, D_MIX, D_MODEL), D_MIX),
        'norm_ffn_g': 1.0 + 0.02 * jax.random.normal(ks[10], (DEPTH, D_MODEL), f32),
        'ffn_w_gate': nrm(ks[11], (N_DENSE_LAYERS, D_MODEL, D_FF_DENSE), D_MODEL),
        'ffn_w_up': nrm(ks[12], (N_DENSE_LAYERS, D_MODEL, D_FF_DENSE), D_MODEL),
        'ffn_w_down': nrm(ks[13], (N_DENSE_LAYERS, D_FF_DENSE, D_MODEL), D_FF_DENSE),
        'w_router': nrm(ks[14], (N_MOE_LAYERS, D_MODEL, N_EXPERTS), D_MODEL),
        'moe_w_gate': nrm(ks[15], (N_MOE_LAYERS, N_EXPERTS, D_MODEL, D_FF_EXPERT), D_MODEL),
        'moe_w_up': nrm(ks[16], (N_MOE_LAYERS, N_EXPERTS, D_MODEL, D_FF_EXPERT), D_MODEL),
        'moe_w_down': nrm(ks[17], (N_MOE_LAYERS, N_EXPERTS, D_FF_EXPERT, D_MODEL), D_FF_EXPERT),
        'norm_final_g': 1.0 + 0.02 * jax.random.normal(ks[18], (D_MODEL,), f32),
    }


def reference(x, norm_mix_g, w_in, b_igate, b_fgate, conv_w, conv_b, g_mlstm, g_sb, w_out,
              norm_ffn_g, ffn_w_gate, ffn_w_up, ffn_w_down, w_router, moe_w_gate, moe_w_up,
              moe_w_down, norm_final_g):
    h = x
    for layer in range(DEPTH):
        y = _rmsnorm(h, norm_mix_g[layer])
        h = h + _hybrid_mixer(y, w_in[layer], b_igate[layer], b_fgate[layer], conv_w[layer],
                              conv_b[layer], g_mlstm[layer], g_sb[layer], w_out[layer])
        y = _rmsnorm(h, norm_ffn_g[layer])
        j = layer // 2
        if layer % 2 == 0:
            h = h + _swiglu(y, ffn_w_gate[j], ffn_w_up[j], ffn_w_down[j])
        else:
            h = h + _moe_swiglu(y, w_router[j], moe_w_gate[j], moe_w_up[j], moe_w_down[j])
    return _rmsnorm(h, norm_final_g)
```

```python
import functools

import jax
import jax.numpy as jnp
from jax import lax
from jax.experimental import pallas as pl
from jax.experimental.pallas import tpu as pltpu

F32 = jnp.float32
BF16 = jnp.bfloat16

D_MODEL = 1024
D_MLSTM = 512
H_MLSTM = 4
DH_MLSTM = 128
D_SB = 512
H_SB = 8
DH_SB = 64
CONV_K = 4
N_EXPERTS = 8
EPS = 1e-6
M_INIT = -1e30

LANE = 128
SUBLANE = 8
CHUNK = 128
VMEM_LIMIT = 52 * 1024 * 1024

C_QK, C_V, C_O, C_QS, C_KS, C_VS, C_GI, C_GF, C_END = (
    0, 1024, 1536, 2048, 2560, 3072, 3584, 3712, 3840)


def _params(sem, **kw):
    return pltpu.CompilerParams(dimension_semantics=sem, vmem_limit_bytes=VMEM_LIMIT, **kw)


def _sigmoid(x):
    return 1.0 / (1.0 + jnp.exp(-x))


def _split3(x):
    a = x.astype(BF16)
    r = x - a.astype(F32)
    b = r.astype(BF16)
    c = (r - b.astype(F32)).astype(BF16)
    return a, b, c


def _dot(a, b):
    return jnp.dot(a, b, preferred_element_type=F32)


def _inproj_kernel(x_ref, g_ref, w_ref, qk_ref, v_ref, o_ref, qs_ref, ks_ref, vs_ref,
                   gi_ref, gf_ref):
    x = x_ref[...]
    ms = jnp.mean(x * x, axis=-1, keepdims=True)
    xn = (x * lax.rsqrt(ms + EPS) * g_ref[...]).astype(BF16)

    def mm(lo, hi):
        return _dot(xn, w_ref[:, lo:hi])

    qk_ref[...] = mm(C_QK, C_V)
    v_ref[...] = mm(C_V, C_O).astype(BF16)
    o_ref[...] = mm(C_O, C_QS)
    qs_ref[...] = (mm(C_QS, C_KS) * (DH_SB ** -0.5)).astype(BF16)
    ks_ref[...] = mm(C_KS, C_VS).astype(BF16)
    vs_ref[...] = mm(C_VS, C_GI).astype(BF16)
    gi_ref[...] = mm(C_GI, C_GF)
    gf_ref[...] = mm(C_GF, C_END)


def _inproj(h, g, w, tm):
    T = h.shape[0]
    row = lambda n: pl.BlockSpec((tm, n), lambda i: (i, 0))
    const = lambda a: pl.BlockSpec(a.shape, lambda i: (0, 0))
    out_shape = (
        jax.ShapeDtypeStruct((T, 1024), F32),
        jax.ShapeDtypeStruct((T, 512), BF16),
        jax.ShapeDtypeStruct((T, 512), F32),
        jax.ShapeDtypeStruct((T, 512), BF16),
        jax.ShapeDtypeStruct((T, 512), BF16),
        jax.ShapeDtypeStruct((T, 512), BF16),
        jax.ShapeDtypeStruct((T, LANE), F32),
        jax.ShapeDtypeStruct((T, LANE), F32),
    )
    return pl.pallas_call(
        _inproj_kernel,
        grid=(T // tm,),
        in_specs=[row(1024), const(g), const(w)],
        out_specs=(row(1024), row(512), row(512), row(512), row(512), row(512),
                   row(LANE), row(LANE)),
        out_shape=out_shape,
        compiler_params=_params(("parallel",)),
        name="inproj",
    )(h, g, w)


def _mlstm_kernel(qk_ref, v_ref, o_ref, gi_ref, gf_ref, cw_ref, cb_ref, bi_ref, bf_ref,
                  gm_ref, tri_ref, out_ref, xpad, cext, mst, *, ts):
    s_idx = pl.program_id(1)

    @pl.when(s_idx == 0)
    def _():
        xpad[0:SUBLANE, :] = jnp.zeros((SUBLANE, 2 * D_MLSTM), F32)
        cext[...] = jnp.zeros(cext.shape, F32)
        mst[...] = jnp.full(mst.shape, M_INIT, F32)

    xpad[SUBLANE:SUBLANE + ts, :] = qk_ref[...]
    y = cb_ref[...]
    for tap in range(CONV_K):
        off = SUBLANE - (CONV_K - 1) + tap
        y = y + xpad[off:off + ts, :] * cw_ref[tap:tap + 1, :]
    xpad[0:SUBLANE, :] = xpad[ts:ts + SUBLANE, :]
    act = y * _sigmoid(y)
    q_all = act[:, :D_MLSTM].astype(BF16)
    kt_all = (act[:, D_MLSTM:] * (DH_MLSTM ** -0.5)).T

    row = lax.broadcasted_iota(jnp.int32, (CHUNK, CHUNK), 0)
    col = lax.broadcasted_iota(jnp.int32, (CHUNK, CHUNK), 1)
    causal = col <= row
    ones_blk = jnp.ones((CHUNK, DH_MLSTM), BF16)
    tri = tri_ref[...]

    for c in range(ts // CHUNK):
        r0 = c * CHUNK
        gi = gi_ref[r0:r0 + CHUNK, :] + bi_ref[...]
        gf = gf_ref[r0:r0 + CHUNK, :] + bf_ref[...]
        lf = jnp.minimum(gf, 0.0) - jnp.log(1.0 + jnp.exp(-jnp.abs(gf)))
        l1, l2, l3 = _split3(lf)
        bcum = _dot(tri, l1) + _dot(tri, l2) + _dot(tri, l3)
        a_all = gi - bcum
        a_t = a_all.T
        for h in range(H_MLSTM):
            hs = slice(h * DH_MLSTM, (h + 1) * DH_MLSTM)
            a_row = a_t[h:h + 1, :]
            m_prev = mst[h:h + 1, 0:1]
            mx = jnp.max(jnp.where(causal, a_row, -jnp.inf), axis=-1, keepdims=True)
            big_m = jnp.maximum(m_prev, mx)
            w = jnp.where(causal, jnp.exp(a_row - big_m), 0.0)
            a_inter = jnp.exp(m_prev - big_m)
            m_t = bcum[:, h:h + 1] + big_m
            qh = q_all[r0:r0 + CHUNK, hs]
            kt = kt_all[hs, r0:r0 + CHUNK]
            sb = (_dot(qh, kt.astype(BF16)) * w).astype(BF16)
            vext = jnp.concatenate([v_ref[r0:r0 + CHUNK, hs], ones_blk], axis=1)
            ce = cext[h]
            numext = a_inter * _dot(qh, ce.astype(BF16)) + _dot(sb, vext)
            num = numext[:, :DH_MLSTM]
            den = numext[:, DH_MLSTM:]
            hh = num / jnp.maximum(jnp.abs(den), jnp.exp(-m_t))
            ms = jnp.mean(hh * hh, axis=-1, keepdims=True)
            yh = hh * lax.rsqrt(ms + EPS) * gm_ref[:, hs]
            out_ref[r0:r0 + CHUNK, hs] = (_sigmoid(o_ref[r0:r0 + CHUNK, hs]) * yh).astype(BF16)
            m_last = big_m[CHUNK - 1:CHUNK, :]
            wkt = (kt * jnp.exp(a_row - m_last)).astype(BF16)
            cext[h] = jnp.exp(m_prev - m_last) * ce + _dot(wkt, vext)
            mst[h:h + 1, :] = jnp.broadcast_to(m_t[CHUNK - 1:CHUNK, :], (1, LANE))


def _mlstm(qk, v, o, gi, gf, conv_w, conv_b, bi, bf, gm, tri, B, S, ts):
    T = B * S
    nsb = S // ts
    row = lambda n: pl.BlockSpec((ts, n), lambda b, s: (b * nsb + s, 0))
    const = lambda a: pl.BlockSpec(a.shape, lambda b, s: (0, 0))
    return pl.pallas_call(
        functools.partial(_mlstm_kernel, ts=ts),
        grid=(B, nsb),
        in_specs=[row(1024), row(512), row(512), row(LANE), row(LANE),
                  const(conv_w), const(conv_b), const(bi), const(bf), const(gm), const(tri)],
        out_specs=row(512),
        out_shape=jax.ShapeDtypeStruct((T, D_MLSTM), BF16),
        scratch_shapes=[
            pltpu.VMEM((ts + SUBLANE, 2 * D_MLSTM), F32),
            pltpu.VMEM((H_MLSTM, DH_MLSTM, 2 * DH_MLSTM), F32),
            pltpu.VMEM((SUBLANE, LANE), F32),
        ],
        compiler_params=_params(("parallel", "arbitrary")),
        name="mlstm",
    )(qk, v, o, gi, gf, conv_w, conv_b, bi, bf, gm, tri)


def _sb_kernel(q_ref, k_ref, v_ref, g_ref, u2_ref, out_ref):
    qi = pl.program_id(2)
    q = q_ref[...]
    lane = lax.broadcasted_iota(jnp.int32, (CHUNK, LANE), 1)
    rowi = lax.broadcasted_iota(jnp.int32, (CHUNK, LANE), 0)
    first = lane < DH_SB
    strict = lane < rowi
    zero = jnp.zeros_like(q)
    qe = (jnp.where(first, q, zero), jnp.where(first, zero, q))
    u2 = u2_ref[...]

    def tile(kb, vb, qh, r, acc, diag):
        z = lax.dot_general(qh, kb, (((1,), (1,)), ((), ())), preferred_element_type=F32)
        sp = jnp.log(1.0 + jnp.exp(-jnp.abs(z)))
        lr = -(jnp.maximum(z, 0.0) + sp)
        lb = lr + z
        if diag:
            lr = jnp.where(strict, lr, 0.0)
        hi = lr.astype(BF16)
        lo = (lr - hi.astype(F32)).astype(BF16)
        x = _dot(jnp.concatenate([hi, lo], axis=1), u2)
        a = jnp.exp(lb + x[:, :LANE] + r)
        if diag:
            a = jnp.where(strict, a, 0.0)
        acc = acc + _dot(a.astype(BF16), vb)
        return r + x[:, LANE:], acc

    start = pl.multiple_of(qi * CHUNK, CHUNK)
    kb = k_ref[pl.ds(start, CHUNK), :]
    vb = v_ref[pl.ds(start, CHUNK), :]
    zf = jnp.zeros((CHUNK, LANE), F32)
    r0, acc0 = tile(kb, vb, qe[0], zf, zf, True)
    r1, acc1 = tile(kb, vb, qe[1], zf, zf, True)

    def body(j, carry):
        r0, acc0, r1, acc1 = carry
        st = pl.multiple_of((qi - 1 - j) * CHUNK, CHUNK)
        kb = k_ref[pl.ds(st, CHUNK), :]
        vb = v_ref[pl.ds(st, CHUNK), :]
        r0, acc0 = tile(kb, vb, qe[0], r0, acc0, False)
        r1, acc1 = tile(kb, vb, qe[1], r1, acc1, False)
        return r0, acc0, r1, acc1

    r0, acc0, r1, acc1 = lax.fori_loop(0, qi, body, (r0, acc0, r1, acc1))
    o = jnp.where(first, acc0, acc1)
    sq = o * o
    s0 = jnp.sum(jnp.where(first, sq, 0.0), axis=-1, keepdims=True)
    s1 = jnp.sum(jnp.where(first, 0.0, sq), axis=-1, keepdims=True)
    ms = jnp.where(first, s0, s1) * (1.0 / DH_SB)
    out_ref[...] = (o * lax.rsqrt(ms + EPS) * g_ref[...]).astype(BF16)


def _sb_attention(qs, ks, vs, g_sb, u2, B, S):
    T = B * S
    nq = S // CHUNK
    npair = D_SB // LANE
    return pl.pallas_call(
        _sb_kernel,
        grid=(B, npair, nq),
        in_specs=[
            pl.BlockSpec((CHUNK, LANE), lambda b, p, i: (b * nq + i, p)),
            pl.BlockSpec((S, LANE), lambda b, p, i: (b, p)),
            pl.BlockSpec((S, LANE), lambda b, p, i: (b, p)),
            pl.BlockSpec((1, LANE), lambda b, p, i: (0, p)),
            pl.BlockSpec(u2.shape, lambda b, p, i: (0, 0)),
        ],
        out_specs=pl.BlockSpec((CHUNK, LANE), lambda b, p, i: (b * nq + i, p)),
        out_shape=jax.ShapeDtypeStruct((T, D_SB), BF16),
        compiler_params=_params(("parallel", "parallel", "arbitrary")),
        name="sb_attention",
    )(qs, ks, vs, g_sb, u2)


def _outproj_kernel(hm_ref, hs_ref, w_ref, h_ref, g_ref, *rest, moe):
    if moe:
        wrh_ref, wrl_ref, hn_ref, yb_ref, yf_ref, ti_ref, tw_ref = rest
    else:
        hn_ref, yb_ref = rest
    hn = h_ref[...] + _dot(hm_ref[...], w_ref[0:D_MLSTM, :]) + _dot(hs_ref[...], w_ref[D_MLSTM:, :])
    hn_ref[...] = hn
    ms = jnp.mean(hn * hn, axis=-1, keepdims=True)
    y = hn * lax.rsqrt(ms + EPS) * g_ref[...]
    yb = y.astype(BF16)
    yb_ref[...] = yb
    if moe:
        yf_ref[...] = yb.astype(F32)
        yl = (y - yb.astype(F32)).astype(BF16)
        logits = _dot(yb, wrh_ref[...]) + _dot(yl, wrh_ref[...]) + _dot(yb, wrl_ref[...])
        lane = lax.broadcasted_iota(jnp.int32, logits.shape, 1)
        lanef = lane.astype(F32)
        lg = jnp.where(lane < N_EXPERTS, logits, -jnp.inf)
        m1 = jnp.max(lg, axis=-1, keepdims=True)
        i1 = jnp.min(jnp.where(lg == m1, lanef, float(LANE)), axis=-1, keepdims=True)
        lg2 = jnp.where(lanef == i1, -jnp.inf, lg)
        m2 = jnp.max(lg2, axis=-1, keepdims=True)
        i2 = jnp.min(jnp.where(lg2 == m2, lanef, float(LANE)), axis=-1, keepdims=True)
        t = jnp.exp(m2 - m1)
        w1 = 1.0 / (1.0 + t)
        w2 = t * w1
        ti_ref[...] = jnp.where(lane == 0, i1, jnp.where(lane == 1, i2, 0.0)).astype(jnp.int32)
        tw_ref[...] = jnp.where(lane == 0, w1, jnp.where(lane == 1, w2, 0.0))


def _outproj(hm, hs, w_out, h, g, tm, router=None):
    T = h.shape[0]
    moe = router is not None
    row = lambda n: pl.BlockSpec((tm, n), lambda i: (i, 0))
    const = lambda a: pl.BlockSpec(a.shape, lambda i: (0, 0))
    ins = [hm, hs, w_out, h, g]
    in_specs = [row(512), row(512), const(w_out), row(1024), const(g)]
    out_shape = [jax.ShapeDtypeStruct((T, D_MODEL), F32), jax.ShapeDtypeStruct((T, D_MODEL), BF16)]
    out_specs = [row(1024), row(1024)]
    if moe:
        ins += list(router)
        in_specs += [const(router[0]), const(router[1])]
        out_shape += [jax.ShapeDtypeStruct((T, D_MODEL), F32),
                      jax.ShapeDtypeStruct((T, LANE), jnp.int32),
                      jax.ShapeDtypeStruct((T, LANE), F32)]
        out_specs += [row(1024), row(LANE), row(LANE)]
    return pl.pallas_call(
        functools.partial(_outproj_kernel, moe=moe),
        grid=(T // tm,),
        in_specs=in_specs,
        out_specs=tuple(out_specs),
        out_shape=tuple(out_shape),
        compiler_params=_params(("parallel",)),
        name="outproj_moe" if moe else "outproj",
    )(*ins)


def _ffn_kernel(y_ref, h_ref, wg_ref, wu_ref, wd_ref, out_ref, *, fc):
    y = y_ref[...]
    out_ref[...] = h_ref[...]
    for c in range(wg_ref.shape[1] // fc):
        cs = slice(c * fc, (c + 1) * fc)
        g = _dot(y, wg_ref[:, cs])
        u = _dot(y, wu_ref[:, cs])
        a = (g * _sigmoid(g) * u).astype(BF16)
        out_ref[...] += _dot(a, wd_ref[cs, :])


def _ffn(y, h, wg, wu, wd, tm, fc):
    T = h.shape[0]
    row = lambda n: pl.BlockSpec((tm, n), lambda i: (i, 0))
    const = lambda a: pl.BlockSpec(a.shape, lambda i: (0, 0))
    return pl.pallas_call(
        functools.partial(_ffn_kernel, fc=fc),
        grid=(T // tm,),
        in_specs=[row(1024), row(1024), const(wg), const(wu), const(wd)],
        out_specs=row(1024),
        out_shape=jax.ShapeDtypeStruct((T, D_MODEL), F32),
        compiler_params=_params(("parallel",)),
        name="ffn",
    )(y, h, wg, wu, wd)


def _row_copy(src, dst, sem, s, d):
    return pltpu.make_async_copy(src.at[pl.ds(s, 1), :], dst.at[pl.ds(d, 1), :], sem)


def _dispatch_kernel(idx_ref, y_hbm, x_hbm, sem, *, rows):
    base = pl.program_id(0) * rows

    def issue(r, c):
        _row_copy(y_hbm, x_hbm, sem, idx_ref[0, 0, r], base + r).start()
        return c

    lax.fori_loop(0, rows, issue, 0)

    def drain(r, c):
        _row_copy(y_hbm, x_hbm, sem, 0, base + r).wait()
        return c

    lax.fori_loop(0, rows, drain, 0)


def _dispatch(y, tok_of_slot, rows):
    P = tok_of_slot.shape[0]
    idx = tok_of_slot.reshape(P // rows, 1, rows)
    return pl.pallas_call(
        functools.partial(_dispatch_kernel, rows=rows),
        grid=(P // rows,),
        in_specs=[pl.BlockSpec((1, 1, rows), lambda i: (i, 0, 0), memory_space=pltpu.SMEM),
                  pl.BlockSpec(memory_space=pl.ANY)],
        out_specs=pl.BlockSpec(memory_space=pl.ANY),
        out_shape=jax.ShapeDtypeStruct((P, D_MODEL), y.dtype),
        scratch_shapes=[pltpu.SemaphoreType.DMA(())],
        compiler_params=_params(("arbitrary",), has_side_effects=True),
        name="moe_dispatch",
    )(idx, y)


def _expert_kernel(te_ref, nu_ref, x_ref, wg_ref, wu_ref, wd_ref, out_ref, xb):
    i = pl.program_id(0)
    f = pl.program_id(1)
    used = i < nu_ref[0]

    @pl.when(f == 0)
    def _():
        xb[...] = x_ref[...].astype(BF16)
        out_ref[...] = jnp.zeros(out_ref.shape, F32)

    @pl.when(used)
    def _():
        x = xb[...]
        g = _dot(x, wg_ref[...])
        u = _dot(x, wu_ref[...])
        a = (g * _sigmoid(g) * u).astype(BF16)
        out_ref[...] += _dot(a, wd_ref[...])


def _experts(x_sorted, tile_expert, n_used, wg, wu, wd, tm, tf):
    P = x_sorted.shape[0]
    F = wg.shape[2]
    grid_spec = pltpu.PrefetchScalarGridSpec(
        num_scalar_prefetch=2,
        grid=(P // tm, F // tf),
        in_specs=[
            pl.BlockSpec((tm, D_MODEL), lambda i, f, te, nu: (i, 0)),
            pl.BlockSpec((None, D_MODEL, tf), lambda i, f, te, nu: (te[i], 0, f)),
            pl.BlockSpec((None, D_MODEL, tf), lambda i, f, te, nu: (te[i], 0, f)),
            pl.BlockSpec((None, tf, D_MODEL), lambda i, f, te, nu: (te[i], f, 0)),
        ],
        out_specs=pl.BlockSpec((tm, D_MODEL), lambda i, f, te, nu: (i, 0)),
        scratch_shapes=[pltpu.VMEM((tm, D_MODEL), BF16)],
    )
    return pl.pallas_call(
        _expert_kernel,
        grid_spec=grid_spec,
        out_shape=jax.ShapeDtypeStruct((P, D_MODEL), F32),
        compiler_params=_params(("parallel", "arbitrary")),
        name="moe_experts",
    )(tile_expert, n_used, x_sorted, wg, wu, wd)


def _combine_kernel(s0_ref, s1_ref, e_hbm, h_ref, tw_ref, out_ref, b0, b1, sem, *, rows):
    def issue(r, c):
        _row_copy(e_hbm, b0, sem.at[0], s0_ref[0, 0, r], r).start()
        _row_copy(e_hbm, b1, sem.at[1], s1_ref[0, 0, r], r).start()
        return c

    lax.fori_loop(0, rows, issue, 0)

    def drain(r, c):
        _row_copy(e_hbm, b0, sem.at[0], 0, r).wait()
        _row_copy(e_hbm, b1, sem.at[1], 0, r).wait()
        return c

    lax.fori_loop(0, rows, drain, 0)
    tw = tw_ref[...]
    out_ref[...] = h_ref[...] + tw[:, 0:1] * b0[...] + tw[:, 1:2] * b1[...]


def _combine(e_sorted, slot0, slot1, h, topw, rows):
    T = h.shape[0]
    n = T // rows
    smem = lambda: pl.BlockSpec((1, 1, rows), lambda i: (i, 0, 0), memory_space=pltpu.SMEM)
    return pl.pallas_call(
        functools.partial(_combine_kernel, rows=rows),
        grid=(n,),
        in_specs=[smem(), smem(), pl.BlockSpec(memory_space=pl.ANY),
                  pl.BlockSpec((rows, D_MODEL), lambda i: (i, 0)),
                  pl.BlockSpec((rows, LANE), lambda i: (i, 0))],
        out_specs=pl.BlockSpec((rows, D_MODEL), lambda i: (i, 0)),
        out_shape=jax.ShapeDtypeStruct((T, D_MODEL), F32),
        scratch_shapes=[pltpu.VMEM((rows, D_MODEL), F32), pltpu.VMEM((rows, D_MODEL), F32),
                        pltpu.SemaphoreType.DMA((2,))],
        compiler_params=_params(("arbitrary",)),
        name="moe_combine",
    )(slot0.reshape(n, 1, rows), slot1.reshape(n, 1, rows), e_sorted, h, topw)


def _route(topi, tm):
    T = topi.shape[0]
    e = topi[:, :2].reshape(-1)
    onehot = (e[:, None] == jnp.arange(N_EXPERTS, dtype=jnp.int32)[None, :]).astype(jnp.int32)
    csum = jnp.cumsum(onehot, axis=0)
    pos = jnp.sum((csum - onehot) * onehot, axis=1)
    counts = csum[-1]
    tiles = (counts + tm - 1) // tm
    tile_end = jnp.cumsum(tiles)
    offs = (tile_end - tiles) * tm
    slot = jnp.sum(onehot * offs[None, :], axis=1) + pos
    n_tiles = 2 * T // tm + N_EXPERTS
    n_used = tile_end[-1:].astype(jnp.int32)
    tile_ids = jnp.arange(n_tiles, dtype=jnp.int32)
    tile_expert = jnp.sum((tile_ids[:, None] >= tile_end[None, :]).astype(jnp.int32), axis=1)
    tile_expert = jnp.minimum(tile_expert, N_EXPERTS - 1).astype(jnp.int32)
    tok_of_slot = jnp.zeros((n_tiles * tm,), jnp.int32).at[slot].set(
        jnp.arange(2 * T, dtype=jnp.int32) // 2)
    slot2 = slot.reshape(T, 2)
    return tok_of_slot, tile_expert, n_used, slot2[:, 0], slot2[:, 1]


def _norm_kernel(x_ref, g_ref, o_ref):
    x = x_ref[...]
    ms = jnp.mean(x * x, axis=-1, keepdims=True)
    o_ref[...] = x * lax.rsqrt(ms + EPS) * g_ref[...]


def _final_norm(h, g, tm):
    T = h.shape[0]
    return pl.pallas_call(
        _norm_kernel,
        grid=(T // tm,),
        in_specs=[pl.BlockSpec((tm, D_MODEL), lambda i: (i, 0)),
                  pl.BlockSpec((1, D_MODEL), lambda i: (0, 0))],
        out_specs=pl.BlockSpec((tm, D_MODEL), lambda i: (i, 0)),
        out_shape=jax.ShapeDtypeStruct((T, D_MODEL), F32),
        compiler_params=_params(("parallel",)),
        name="final_norm",
    )(h, g)


def _pad_cols(w, n):
    return jnp.pad(w, ((0, 0), (0, n - w.shape[1])))


def _prep_w_in(w):
    gates0 = 4 * D_MLSTM
    gi = _pad_cols(w[:, gates0:gates0 + H_MLSTM], LANE)
    gf = _pad_cols(w[:, gates0 + H_MLSTM:gates0 + 2 * H_MLSTM], LANE)
    rest = w[:, gates0 + 2 * H_MLSTM:]
    return jnp.concatenate([w[:, :gates0], rest, gi, gf], axis=1).astype(BF16)


def kernel(x, norm_mix_g, w_in, b_igate, b_fgate, conv_w, conv_b, g_mlstm, g_sb, w_out,
           norm_ffn_g, ffn_w_gate, ffn_w_up, ffn_w_down, w_router, moe_w_gate, moe_w_up,
           moe_w_down, norm_final_g):
    B, S, D = x.shape
    T = B * S
    depth = w_in.shape[0]
    tm = min(512, T)
    ts = min(256, S)
    tm_moe = min(512, T)
    rows = min(256, T)

    ii = jnp.arange(CHUNK)
    tri = (ii[None, :] <= ii[:, None]).astype(BF16)
    upper = (ii[:, None] > ii[None, :]).astype(BF16)
    u1 = jnp.concatenate([upper, jnp.ones((CHUNK, CHUNK), BF16)], axis=1)
    u2 = jnp.concatenate([u1, u1], axis=0)

    f_dense = ffn_w_gate.shape[2]
    f_pad = -(-f_dense // 256) * 256

    h = x.reshape(T, D)
    for layer in range(depth):
        w1 = _prep_w_in(w_in[layer])
        qk, v_m, o_m, q_s, k_s, v_s, gi, gf = _inproj(h, norm_mix_g[layer][None, :], w1, tm)
        bi = _pad_cols(b_igate[layer][None, :], LANE)
        bf = _pad_cols(b_fgate[layer][None, :], LANE)
        h_m = _mlstm(qk, v_m, o_m, gi, gf, conv_w[layer], conv_b[layer][None, :], bi, bf,
                     g_mlstm[layer][None, :], tri, B, S, ts)
        h_s = _sb_attention(q_s, k_s, v_s, g_sb[layer][None, :], u2, B, S)
        wo = w_out[layer].astype(BF16)
        gffn = norm_ffn_g[layer][None, :]
        j = layer // 2
        if layer % 2 == 0:
            h, yb = _outproj(h_m, h_s, wo, h, gffn, tm)
            wg = _pad_cols(ffn_w_gate[j], f_pad).astype(BF16)
            wu = _pad_cols(ffn_w_up[j], f_pad).astype(BF16)
            wd = jnp.pad(ffn_w_down[j], ((0, f_pad - f_dense), (0, 0))).astype(BF16)
            h = _ffn(yb, h, wg, wu, wd, tm, 256)
        else:
            wr = _pad_cols(w_router[j], LANE)
            wrh = wr.astype(BF16)
            wrl = (wr - wrh.astype(F32)).astype(BF16)
            h, yb, yf, topi, topw = _outproj(h_m, h_s, wo, h, gffn, tm, router=(wrh, wrl))
            tok_of_slot, tile_expert, n_used, slot0, slot1 = _route(topi, tm_moe)
            x_sorted = _dispatch(yf, tok_of_slot, rows)
            e_sorted = _experts(x_sorted, tile_expert, n_used, moe_w_gate[j].astype(BF16),
                                moe_w_up[j].astype(BF16), moe_w_down[j].astype(BF16),
                                tm_moe, 512)
            h = _combine(e_sorted, slot0, slot1, h, topw, rows)
    out = _final_norm(h, norm_final_g[None, :], tm)
    return out.reshape(B, S, D)
```

```python
import functools

import jax
import jax.numpy as jnp
from jax import lax
from jax.experimental import pallas as pl
from jax.experimental.pallas import tpu as pltpu

F32 = jnp.float32
BF16 = jnp.bfloat16

D_MODEL = 1024
D_MLSTM = 512
H_MLSTM = 4
DH_MLSTM = 128
D_SB = 512
H_SB = 8
DH_SB = 64
CONV_K = 4
N_EXPERTS = 8
EPS = 1e-6
M_INIT = -1e30

LANE = 128
SUBLANE = 8
CHUNK = 128
VMEM_LIMIT = 52 * 1024 * 1024

C_QK, C_V, C_O, C_QS, C_KS, C_VS, C_GI, C_GF, C_END = (
    0, 1024, 1536, 2048, 2560, 3072, 3584, 3712, 3840)


def _params(sem, **kw):
    return pltpu.CompilerParams(dimension_semantics=sem, vmem_limit_bytes=VMEM_LIMIT, **kw)


def _sigmoid(x):
    return 1.0 / (1.0 + jnp.exp(-x))


def _split3(x):
    a = x.astype(BF16)
    r = x - a.astype(F32)
    b = r.astype(BF16)
    c = (r - b.astype(F32)).astype(BF16)
    return a, b, c


def _dot(a, b):
    return jnp.dot(a, b, preferred_element_type=F32)


def _inproj_kernel(x_ref, g_ref, w_ref, qk_ref, v_ref, o_ref, qs_ref, ks_ref, vs_ref,
                   gi_ref, gf_ref):
    x = x_ref[...]
    ms = jnp.mean(x * x, axis=-1, keepdims=True)
    xn = (x * lax.rsqrt(ms + EPS) * g_ref[...]).astype(BF16)

    def mm(lo, hi):
        return _dot(xn, w_ref[:, lo:hi])

    qk_ref[...] = mm(C_QK, C_V)
    v_ref[...] = mm(C_V, C_O).astype(BF16)
    o_ref[...] = mm(C_O, C_QS)
    qs_ref[...] = (mm(C_QS, C_KS) * (DH_SB ** -0.5)).astype(BF16)
    ks_ref[...] = mm(C_KS, C_VS).astype(BF16)
    vs_ref[...] = mm(C_VS, C_GI).astype(BF16)
    gi_ref[...] = mm(C_GI, C_GF)
    gf_ref[...] = mm(C_GF, C_END)


def _inproj(h, g, w, tm):
    T = h.shape[0]
    row = lambda n: pl.BlockSpec((tm, n), lambda i: (i, 0))
    const = lambda a: pl.BlockSpec(a.shape, lambda i: (0, 0))
    out_shape = (
        jax.ShapeDtypeStruct((T, 1024), F32),
        jax.ShapeDtypeStruct((T, 512), BF16),
        jax.ShapeDtypeStruct((T, 512), F32),
        jax.ShapeDtypeStruct((T, 512), BF16),
        jax.ShapeDtypeStruct((T, 512), BF16),
        jax.ShapeDtypeStruct((T, 512), BF16),
        jax.ShapeDtypeStruct((T, LANE), F32),
        jax.ShapeDtypeStruct((T, LANE), F32),
    )
    return pl.pallas_call(
        _inproj_kernel,
        grid=(T // tm,),
        in_specs=[row(1024), const(g), const(w)],
        out_specs=(row(1024), row(512), row(512), row(512), row(512), row(512),
                   row(LANE), row(LANE)),
        out_shape=out_shape,
        compiler_params=_params(("parallel",)),
        name="inproj",
    )(h, g, w)


def _mlstm_kernel(qk_ref, v_ref, o_ref, gi_ref, gf_ref, cw_ref, cb_ref, bi_ref, bf_ref,
                  gm_ref, tri_ref, out_ref, xpad, cext, mst, *, ts):
    s_idx = pl.program_id(1)

    @pl.when(s_idx == 0)
    def _():
        xpad[0:SUBLANE, :] = jnp.zeros((SUBLANE, 2 * D_MLSTM), F32)
        cext[...] = jnp.zeros(cext.shape, F32)
        mst[...] = jnp.full(mst.shape, M_INIT, F32)

    xpad[SUBLANE:SUBLANE + ts, :] = qk_ref[...]
    y = cb_ref[...]
    for tap in range(CONV_K):
        off = SUBLANE - (CONV_K - 1) + tap
        y = y + xpad[off:off + ts, :] * cw_ref[tap:tap + 1, :]
    xpad[0:SUBLANE, :] = xpad[ts:ts + SUBLANE, :]
    act = y * _sigmoid(y)
    q_all = act[:, :D_MLSTM].astype(BF16)
    kt_all = (act[:, D_MLSTM:] * (DH_MLSTM ** -0.5)).T

    row = lax.broadcasted_iota(jnp.int32, (CHUNK, CHUNK), 0)
    col = lax.broadcasted_iota(jnp.int32, (CHUNK, CHUNK), 1)
    causal = col <= row
    ones_blk = jnp.ones((CHUNK, DH_MLSTM), BF16)
    tri = tri_ref[...]

    for c in range(ts // CHUNK):
        r0 = c * CHUNK
        gi = gi_ref[r0:r0 + CHUNK, :] + bi_ref[...]
        gf = gf_ref[r0:r0 + CHUNK, :] + bf_ref[...]
        lf = jnp.minimum(gf, 0.0) - jnp.log(1.0 + jnp.exp(-jnp.abs(gf)))
        l1, l2, l3 = _split3(lf)
        bcum = _dot(tri, l1) + _dot(tri, l2) + _dot(tri, l3)
        a_all = gi - bcum
        a_t = a_all.T
        for h in range(H_MLSTM):
            hs = slice(h * DH_MLSTM, (h + 1) * DH_MLSTM)
            a_row = a_t[h:h + 1, :]
            m_prev = mst[h:h + 1, 0:1]
            mx = jnp.max(jnp.where(causal, a_row, -jnp.inf), axis=-1, keepdims=True)
            big_m = jnp.maximum(m_prev, mx)
            w = jnp.where(causal, jnp.exp(a_row - big_m), 0.0)
            a_inter = jnp.exp(m_prev - big_m)
            m_t = bcum[:, h:h + 1] + big_m
            qh = q_all[r0:r0 + CHUNK, hs]
            kt = kt_all[hs, r0:r0 + CHUNK]
            sb = (_dot(qh, kt.astype(BF16)) * w).astype(BF16)
            vext = jnp.concatenate([v_ref[r0:r0 + CHUNK, hs], ones_blk], axis=1)
            ce = cext[h]
            numext = a_inter * _dot(qh, ce.astype(BF16)) + _dot(sb, vext)
            num = numext[:, :DH_MLSTM]
            den = numext[:, DH_MLSTM:]
            hh = num / jnp.maximum(jnp.abs(den), jnp.exp(-m_t))
            ms = jnp.mean(hh * hh, axis=-1, keepdims=True)
            yh = hh * lax.rsqrt(ms + EPS) * gm_ref[:, hs]
            out_ref[r0:r0 + CHUNK, hs] = (_sigmoid(o_ref[r0:r0 + CHUNK, hs]) * yh).astype(BF16)
            m_last = big_m[CHUNK - 1:CHUNK, :]
            wkt = (kt * jnp.exp(a_row - m_last)).astype(BF16)
            cext[h] = jnp.exp(m_prev - m_last) * ce + _dot(wkt, vext)
            mst[h:h + 1, :] = jnp.broadcast_to(m_t[CHUNK - 1:CHUNK, :], (1, LANE))


def _mlstm(qk, v, o, gi, gf, conv_w, conv_b, bi, bf, gm, tri, B, S, ts):
    T = B * S
    nsb = S // ts
    row = lambda n: pl.BlockSpec((ts, n), lambda b, s: (b * nsb + s, 0))
    const = lambda a: pl.BlockSpec(a.shape, lambda b, s: (0, 0))
    return pl.pallas_call(
        functools.partial(_mlstm_kernel, ts=ts),
        grid=(B, nsb),
        in_specs=[row(1024), row(512), row(512), row(LANE), row(LANE),
                  const(conv_w), const(conv_b), const(bi), const(bf), const(gm), const(tri)],
        out_specs=row(512),
        out_shape=jax.ShapeDtypeStruct((T, D_MLSTM), BF16),
        scratch_shapes=[
            pltpu.VMEM((ts + SUBLANE, 2 * D_MLSTM), F32),
            pltpu.VMEM((H_MLSTM, DH_MLSTM, 2 * DH_MLSTM), F32),
            pltpu.VMEM((SUBLANE, LANE), F32),
        ],
        compiler_params=_params(("parallel", "arbitrary")),
        name="mlstm",
    )(qk, v, o, gi, gf, conv_w, conv_b, bi, bf, gm, tri)


def _sb_kernel(q_ref, k_ref, v_ref, g_ref, u_ref, out_ref, r_scr, acc_scr, *, tq, npb):
    qi = pl.program_id(2)
    lane = lax.broadcasted_iota(jnp.int32, (tq, LANE), 1)
    first = lane < DH_SB
    r_scr[...] = jnp.zeros(r_scr.shape, F32)
    acc_scr[...] = jnp.zeros(acc_scr.shape, F32)

    def head_q(p, e):
        q = q_ref[:, p * LANE:(p + 1) * LANE]
        zero = jnp.zeros_like(q)
        return jnp.where(first, q, zero) if e == 0 else jnp.where(first, zero, q)

    def group(g, p, e, r, diag):
        st = pl.multiple_of(g * tq, tq)
        kb = k_ref[pl.ds(st, tq), p * LANE:(p + 1) * LANE]
        vb = v_ref[pl.ds(st, tq), p * LANE:(p + 1) * LANE]
        z = lax.dot_general(head_q(p, e), kb, (((1,), (1,)), ((), ())),
                            preferred_element_type=F32)
        sp = jnp.log(1.0 + jnp.exp(-jnp.abs(z)))
        lr = -(jnp.maximum(z, 0.0) + sp)
        lb = lr + z
        if diag:
            strict = (lax.broadcasted_iota(jnp.int32, (tq, tq), 1)
                      < lax.broadcasted_iota(jnp.int32, (tq, tq), 0))
            lr = jnp.where(strict, lr, 0.0)
        x = _dot(lr.astype(BF16), u_ref[...])
        a = jnp.exp(lb + x[:, :tq] + jnp.concatenate([r] * (tq // LANE), axis=1))
        if diag:
            a = jnp.where(strict, a, 0.0)
        return r + x[:, tq:], _dot(a.astype(BF16), vb)

    def sweep(g, diag):
        for p in range(npb):
            for e in range(2):
                i = 2 * p + e
                r, c = group(g, p, e, r_scr[i], diag)
                r_scr[i] = r
                acc_scr[i] += c

    sweep(qi, True)

    def body(j, c):
        sweep(qi - 1 - j, False)
        return c

    lax.fori_loop(0, qi, body, 0)

    for p in range(npb):
        o = jnp.where(first, acc_scr[2 * p], acc_scr[2 * p + 1])
        sq = o * o
        s0 = jnp.sum(jnp.where(first, sq, 0.0), axis=-1, keepdims=True)
        s1 = jnp.sum(jnp.where(first, 0.0, sq), axis=-1, keepdims=True)
        ms = jnp.where(first, s0, s1) * (1.0 / DH_SB)
        ls = slice(p * LANE, (p + 1) * LANE)
        out_ref[:, ls] = (o * lax.rsqrt(ms + EPS) * g_ref[:, ls]).astype(BF16)


def _sb_attention(qs, ks, vs, g_sb, usuf, B, S, tq, npb):
    T = B * S
    nq = S // tq
    w = npb * LANE
    return pl.pallas_call(
        functools.partial(_sb_kernel, tq=tq, npb=npb),
        grid=(B, D_SB // w, nq),
        in_specs=[
            pl.BlockSpec((tq, w), lambda b, p, i: (b * nq + i, p)),
            pl.BlockSpec((S, w), lambda b, p, i: (b, p)),
            pl.BlockSpec((S, w), lambda b, p, i: (b, p)),
            pl.BlockSpec((1, w), lambda b, p, i: (0, p)),
            pl.BlockSpec(usuf.shape, lambda b, p, i: (0, 0)),
        ],
        out_specs=pl.BlockSpec((tq, w), lambda b, p, i: (b * nq + i, p)),
        out_shape=jax.ShapeDtypeStruct((T, D_SB), BF16),
        scratch_shapes=[pltpu.VMEM((2 * npb, tq, LANE), F32),
                        pltpu.VMEM((2 * npb, tq, LANE), F32)],
        compiler_params=_params(("parallel", "parallel", "arbitrary")),
        name="sb_attention",
    )(qs, ks, vs, g_sb, usuf)


def _outproj_kernel(hm_ref, hs_ref, w_ref, h_ref, g_ref, *rest, moe):
    if moe:
        wrh_ref, wrl_ref, hn_ref, yb_ref, yf_ref, ti_ref, tw_ref = rest
    else:
        hn_ref, yb_ref = rest
    hn = h_ref[...] + _dot(hm_ref[...], w_ref[0:D_MLSTM, :]) + _dot(hs_ref[...], w_ref[D_MLSTM:, :])
    hn_ref[...] = hn
    ms = jnp.mean(hn * hn, axis=-1, keepdims=True)
    y = hn * lax.rsqrt(ms + EPS) * g_ref[...]
    yb = y.astype(BF16)
    yb_ref[...] = yb
    if moe:
        yf_ref[...] = yb.astype(F32)
        yl = (y - yb.astype(F32)).astype(BF16)
        logits = _dot(yb, wrh_ref[...]) + _dot(yl, wrh_ref[...]) + _dot(yb, wrl_ref[...])
        lane = lax.broadcasted_iota(jnp.int32, logits.shape, 1)
        lanef = lane.astype(F32)
        lg = jnp.where(lane < N_EXPERTS, logits, -jnp.inf)
        m1 = jnp.max(lg, axis=-1, keepdims=True)
        i1 = jnp.min(jnp.where(lg == m1, lanef, float(LANE)), axis=-1, keepdims=True)
        lg2 = jnp.where(lanef == i1, -jnp.inf, lg)
        m2 = jnp.max(lg2, axis=-1, keepdims=True)
        i2 = jnp.min(jnp.where(lg2 == m2, lanef, float(LANE)), axis=-1, keepdims=True)
        t = jnp.exp(m2 - m1)
        w1 = 1.0 / (1.0 + t)
        w2 = t * w1
        ti_ref[...] = jnp.where(lane == 0, i1, jnp.where(lane == 1, i2, 0.0)).astype(jnp.int32)
        tw_ref[...] = jnp.where(lane == 0, w1, jnp.where(lane == 1, w2, 0.0))


def _outproj(hm, hs, w_out, h, g, tm, router=None):
    T = h.shape[0]
    moe = router is not None
    row = lambda n: pl.BlockSpec((tm, n), lambda i: (i, 0))
    const = lambda a: pl.BlockSpec(a.shape, lambda i: (0, 0))
    ins = [hm, hs, w_out, h, g]
    in_specs = [row(512), row(512), const(w_out), row(1024), const(g)]
    out_shape = [jax.ShapeDtypeStruct((T, D_MODEL), F32), jax.ShapeDtypeStruct((T, D_MODEL), BF16)]
    out_specs = [row(1024), row(1024)]
    if moe:
        ins += list(router)
        in_specs += [const(router[0]), const(router[1])]
        out_shape += [jax.ShapeDtypeStruct((T, D_MODEL), F32),
                      jax.ShapeDtypeStruct((T, LANE), jnp.int32),
                      jax.ShapeDtypeStruct((T, LANE), F32)]
        out_specs += [row(1024), row(LANE), row(LANE)]
    return pl.pallas_call(
        functools.partial(_outproj_kernel, moe=moe),
        grid=(T // tm,),
        in_specs=in_specs,
        out_specs=tuple(out_specs),
        out_shape=tuple(out_shape),
        compiler_params=_params(("parallel",)),
        name="outproj_moe" if moe else "outproj",
    )(*ins)


def _ffn_kernel(y_ref, h_ref, wg_ref, wu_ref, wd_ref, out_ref, *, fc):
    y = y_ref[...]
    out_ref[...] = h_ref[...]
    for c in range(wg_ref.shape[1] // fc):
        cs = slice(c * fc, (c + 1) * fc)
        g = _dot(y, wg_ref[:, cs])
        u = _dot(y, wu_ref[:, cs])
        a = (g * _sigmoid(g) * u).astype(BF16)
        out_ref[...] += _dot(a, wd_ref[cs, :])


def _ffn(y, h, wg, wu, wd, tm, fc):
    T = h.shape[0]
    row = lambda n: pl.BlockSpec((tm, n), lambda i: (i, 0))
    const = lambda a: pl.BlockSpec(a.shape, lambda i: (0, 0))
    return pl.pallas_call(
        functools.partial(_ffn_kernel, fc=fc),
        grid=(T // tm,),
        in_specs=[row(1024), row(1024), const(wg), const(wu), const(wd)],
        out_specs=row(1024),
        out_shape=jax.ShapeDtypeStruct((T, D_MODEL), F32),
        compiler_params=_params(("parallel",)),
        name="ffn",
    )(y, h, wg, wu, wd)


def _row_copy(src, dst, sem, s, d):
    return pltpu.make_async_copy(src.at[pl.ds(s, 1), :], dst.at[pl.ds(d, 1), :], sem)


def _dispatch_kernel(idx_ref, y_hbm, x_ref, buf, sem, *, rows):
    def issue(r, c):
        _row_copy(y_hbm, buf, sem, idx_ref[0, 0, r], r).start()
        return c

    lax.fori_loop(0, rows, issue, 0, unroll=8)

    def drain(r, c):
        _row_copy(y_hbm, buf, sem, 0, r).wait()
        return c

    lax.fori_loop(0, rows, drain, 0, unroll=8)
    x_ref[...] = buf[...].astype(BF16)


def _dispatch(y, tok_of_slot, rows):
    P = tok_of_slot.shape[0]
    idx = tok_of_slot.reshape(P // rows, 1, rows)
    return pl.pallas_call(
        functools.partial(_dispatch_kernel, rows=rows),
        grid=(P // rows,),
        in_specs=[pl.BlockSpec((1, 1, rows), lambda i: (i, 0, 0), memory_space=pltpu.SMEM),
                  pl.BlockSpec(memory_space=pl.ANY)],
        out_specs=pl.BlockSpec((rows, D_MODEL), lambda i: (i, 0)),
        out_shape=jax.ShapeDtypeStruct((P, D_MODEL), BF16),
        scratch_shapes=[pltpu.VMEM((rows, D_MODEL), F32), pltpu.SemaphoreType.DMA(())],
        compiler_params=_params(("arbitrary",)),
        name="moe_dispatch",
    )(idx, y)


def _expert_kernel(te_ref, nu_ref, x_ref, wg_ref, wu_ref, wd_ref, out_ref):
    i = pl.program_id(0)
    f = pl.program_id(1)
    used = i < nu_ref[0]

    @pl.when(f == 0)
    def _():
        out_ref[...] = jnp.zeros(out_ref.shape, F32)

    @pl.when(used)
    def _():
        x = x_ref[...]
        g = _dot(x, wg_ref[...].astype(BF16))
        u = _dot(x, wu_ref[...].astype(BF16))
        a = (g * _sigmoid(g) * u).astype(BF16)
        out_ref[...] += _dot(a, wd_ref[...].astype(BF16))


def _experts(x_sorted, tile_expert, n_used, wg, wu, wd, tm, tf):
    P = x_sorted.shape[0]
    F = wg.shape[2]
    grid_spec = pltpu.PrefetchScalarGridSpec(
        num_scalar_prefetch=2,
        grid=(P // tm, F // tf),
        in_specs=[
            pl.BlockSpec((tm, D_MODEL), lambda i, f, te, nu: (i, 0)),
            pl.BlockSpec((None, D_MODEL, tf), lambda i, f, te, nu: (te[i], 0, f)),
            pl.BlockSpec((None, D_MODEL, tf), lambda i, f, te, nu: (te[i], 0, f)),
            pl.BlockSpec((None, tf, D_MODEL), lambda i, f, te, nu: (te[i], f, 0)),
        ],
        out_specs=pl.BlockSpec((tm, D_MODEL), lambda i, f, te, nu: (i, 0)),
    )
    return pl.pallas_call(
        _expert_kernel,
        grid_spec=grid_spec,
        out_shape=jax.ShapeDtypeStruct((P, D_MODEL), F32),
        compiler_params=_params(("parallel", "arbitrary")),
        name="moe_experts",
    )(tile_expert, n_used, x_sorted, wg, wu, wd)


def _combine_kernel(s0_ref, s1_ref, e_hbm, h_ref, tw_ref, out_ref, b0, b1, sem, *, rows):
    def issue(r, c):
        _row_copy(e_hbm, b0, sem.at[0], s0_ref[0, 0, r], r).start()
        _row_copy(e_hbm, b1, sem.at[1], s1_ref[0, 0, r], r).start()
        return c

    lax.fori_loop(0, rows, issue, 0, unroll=8)

    def drain(r, c):
        _row_copy(e_hbm, b0, sem.at[0], 0, r).wait()
        _row_copy(e_hbm, b1, sem.at[1], 0, r).wait()
        return c

    lax.fori_loop(0, rows, drain, 0, unroll=8)
    tw = tw_ref[...]
    out_ref[...] = h_ref[...] + tw[:, 0:1] * b0[...] + tw[:, 1:2] * b1[...]


def _combine(e_sorted, slot0, slot1, h, topw, rows):
    T = h.shape[0]
    n = T // rows
    smem = lambda: pl.BlockSpec((1, 1, rows), lambda i: (i, 0, 0), memory_space=pltpu.SMEM)
    return pl.pallas_call(
        functools.partial(_combine_kernel, rows=rows),
        grid=(n,),
        in_specs=[smem(), smem(), pl.BlockSpec(memory_space=pl.ANY),
                  pl.BlockSpec((rows, D_MODEL), lambda i: (i, 0)),
                  pl.BlockSpec((rows, LANE), lambda i: (i, 0))],
        out_specs=pl.BlockSpec((rows, D_MODEL), lambda i: (i, 0)),
        out_shape=jax.ShapeDtypeStruct((T, D_MODEL), F32),
        scratch_shapes=[pltpu.VMEM((rows, D_MODEL), F32), pltpu.VMEM((rows, D_MODEL), F32),
                        pltpu.SemaphoreType.DMA((2,))],
        compiler_params=_params(("arbitrary",)),
        name="moe_combine",
    )(slot0.reshape(n, 1, rows), slot1.reshape(n, 1, rows), e_sorted, h, topw)


def _route(topi, tm):
    T = topi.shape[0]
    e = topi[:, :2].reshape(-1)
    onehot = (e[:, None] == jnp.arange(N_EXPERTS, dtype=jnp.int32)[None, :]).astype(jnp.int32)
    csum = jnp.cumsum(onehot, axis=0)
    pos = jnp.sum((csum - onehot) * onehot, axis=1)
    counts = csum[-1]
    tiles = (counts + tm - 1) // tm
    tile_end = jnp.cumsum(tiles)
    offs = (tile_end - tiles) * tm
    slot = jnp.sum(onehot * offs[None, :], axis=1) + pos
    n_tiles = 2 * T // tm + N_EXPERTS
    n_used = tile_end[-1:].astype(jnp.int32)
    tile_ids = jnp.arange(n_tiles, dtype=jnp.int32)
    tile_expert = jnp.sum((tile_ids[:, None] >= tile_end[None, :]).astype(jnp.int32), axis=1)
    tile_expert = jnp.minimum(tile_expert, N_EXPERTS - 1).astype(jnp.int32)
    tok_of_slot = jnp.zeros((n_tiles * tm,), jnp.int32).at[slot].set(
        jnp.arange(2 * T, dtype=jnp.int32) // 2)
    slot2 = slot.reshape(T, 2)
    return tok_of_slot, tile_expert, n_used, slot2[:, 0], slot2[:, 1]


def _norm_kernel(x_ref, g_ref, o_ref):
    x = x_ref[...]
    ms = jnp.mean(x * x, axis=-1, keepdims=True)
    o_ref[...] = x * lax.rsqrt(ms + EPS) * g_ref[...]


def _final_norm(h, g, tm):
    T = h.shape[0]
    return pl.pallas_call(
        _norm_kernel,
        grid=(T // tm,),
        in_specs=[pl.BlockSpec((tm, D_MODEL), lambda i: (i, 0)),
                  pl.BlockSpec((1, D_MODEL), lambda i: (0, 0))],
        out_specs=pl.BlockSpec((tm, D_MODEL), lambda i: (i, 0)),
        out_shape=jax.ShapeDtypeStruct((T, D_MODEL), F32),
        compiler_params=_params(("parallel",)),
        name="final_norm",
    )(h, g)


def _pad_cols(w, n):
    return jnp.pad(w, ((0, 0), (0, n - w.shape[1])))


def _prep_w_in(w):
    gates0 = 4 * D_MLSTM
    gi = _pad_cols(w[:, gates0:gates0 + H_MLSTM], LANE)
    gf = _pad_cols(w[:, gates0 + H_MLSTM:gates0 + 2 * H_MLSTM], LANE)
    rest = w[:, gates0 + 2 * H_MLSTM:]
    return jnp.concatenate([w[:, :gates0], rest, gi, gf], axis=1).astype(BF16)


def kernel(x, norm_mix_g, w_in, b_igate, b_fgate, conv_w, conv_b, g_mlstm, g_sb, w_out,
           norm_ffn_g, ffn_w_gate, ffn_w_up, ffn_w_down, w_router, moe_w_gate, moe_w_up,
           moe_w_down, norm_final_g):
    B, S, D = x.shape
    T = B * S
    depth = w_in.shape[0]
    tm = min(512, T)
    ts = min(256, S)
    tq = min(256, S)
    tm_moe = min(1024, T)
    rows = min(256, T)

    ii = jnp.arange(CHUNK)
    tri = (ii[None, :] <= ii[:, None]).astype(BF16)
    jj = jnp.arange(tq)
    upper = (jj[:, None] > jj[None, :]).astype(BF16)
    usuf = jnp.concatenate([upper, jnp.ones((tq, LANE), BF16)], axis=1)

    f_dense = ffn_w_gate.shape[2]
    f_pad = -(-f_dense // 256) * 256

    h = x.reshape(T, D)
    for layer in range(depth):
        w1 = _prep_w_in(w_in[layer])
        qk, v_m, o_m, q_s, k_s, v_s, gi, gf = _inproj(h, norm_mix_g[layer][None, :], w1, tm)
        bi = _pad_cols(b_igate[layer][None, :], LANE)
        bf = _pad_cols(b_fgate[layer][None, :], LANE)
        h_m = _mlstm(qk, v_m, o_m, gi, gf, conv_w[layer], conv_b[layer][None, :], bi, bf,
                     g_mlstm[layer][None, :], tri, B, S, ts)
        h_s = _sb_attention(q_s, k_s, v_s, g_sb[layer][None, :], usuf, B, S, tq, 4)
        wo = w_out[layer].astype(BF16)
        gffn = norm_ffn_g[layer][None, :]
        j = layer // 2
        if layer % 2 == 0:
            h, yb = _outproj(h_m, h_s, wo, h, gffn, tm)
            wg = _pad_cols(ffn_w_gate[j], f_pad).astype(BF16)
            wu = _pad_cols(ffn_w_up[j], f_pad).astype(BF16)
            wd = jnp.pad(ffn_w_down[j], ((0, f_pad - f_dense), (0, 0))).astype(BF16)
            h = _ffn(yb, h, wg, wu, wd, tm, 256)
        else:
            wr = _pad_cols(w_router[j], LANE)
            wrh = wr.astype(BF16)
            wrl = (wr - wrh.astype(F32)).astype(BF16)
            h, yb, yf, topi, topw = _outproj(h_m, h_s, wo, h, gffn, tm, router=(wrh, wrl))
            tok_of_slot, tile_expert, n_used, slot0, slot1 = _route(topi, tm_moe)
            x_sorted = _dispatch(yf, tok_of_slot, rows)
            e_sorted = _experts(x_sorted, tile_expert, n_used, moe_w_gate[j], moe_w_up[j],
                                moe_w_down[j], tm_moe, 512)
            h = _combine(e_sorted, slot0, slot1, h, topw, rows)
    out = _final_norm(h, norm_final_g[None, :], tm)
    return out.reshape(B, S, D)
```

```python
import functools

import jax
import jax.numpy as jnp
from jax import lax
from jax.experimental import pallas as pl
from jax.experimental.pallas import tpu as pltpu

F32 = jnp.float32
BF16 = jnp.bfloat16

D_MODEL = 1024
D_MLSTM = 512
H_MLSTM = 4
DH_MLSTM = 128
D_SB = 512
H_SB = 8
DH_SB = 64
CONV_K = 4
N_EXPERTS = 8
EPS = 1e-6
M_INIT = -1e30

LANE = 128
SUBLANE = 8
CHUNK = 128
VMEM_LIMIT = 52 * 1024 * 1024

C_QK, C_V, C_O, C_QS, C_KS, C_VS, C_GI, C_GF, C_END = (
    0, 1024, 1536, 2048, 2560, 3072, 3584, 3712, 3840)


def _params(sem, **kw):
    return pltpu.CompilerParams(dimension_semantics=sem, vmem_limit_bytes=VMEM_LIMIT, **kw)


def _sigmoid(x):
    return 1.0 / (1.0 + jnp.exp(-x))


def _split3(x):
    a = x.astype(BF16)
    r = x - a.astype(F32)
    b = r.astype(BF16)
    c = (r - b.astype(F32)).astype(BF16)
    return a, b, c


def _dot(a, b):
    return jnp.dot(a, b, preferred_element_type=F32)


def _inproj_kernel(x_ref, g_ref, w_ref, qk_ref, v_ref, o_ref, qs_ref, ks_ref, vs_ref,
                   gi_ref, gf_ref):
    x = x_ref[...]
    ms = jnp.mean(x * x, axis=-1, keepdims=True)
    xn = (x * lax.rsqrt(ms + EPS) * g_ref[...]).astype(BF16)

    def mm(lo, hi):
        return _dot(xn, w_ref[:, lo:hi])

    qk_ref[...] = mm(C_QK, C_V)
    v_ref[...] = mm(C_V, C_O).astype(BF16)
    o_ref[...] = mm(C_O, C_QS)
    qs_ref[...] = (mm(C_QS, C_KS) * (DH_SB ** -0.5)).astype(BF16)
    ks_ref[...] = mm(C_KS, C_VS).astype(BF16)
    vs_ref[...] = mm(C_VS, C_GI).astype(BF16)
    gi_ref[...] = mm(C_GI, C_GF)
    gf_ref[...] = mm(C_GF, C_END)


def _inproj(h, g, w, tm):
    T = h.shape[0]
    row = lambda n: pl.BlockSpec((tm, n), lambda i: (i, 0))
    const = lambda a: pl.BlockSpec(a.shape, lambda i: (0, 0))
    out_shape = (
        jax.ShapeDtypeStruct((T, 1024), F32),
        jax.ShapeDtypeStruct((T, 512), BF16),
        jax.ShapeDtypeStruct((T, 512), F32),
        jax.ShapeDtypeStruct((T, 512), BF16),
        jax.ShapeDtypeStruct((T, 512), BF16),
        jax.ShapeDtypeStruct((T, 512), BF16),
        jax.ShapeDtypeStruct((T, LANE), F32),
        jax.ShapeDtypeStruct((T, LANE), F32),
    )
    return pl.pallas_call(
        _inproj_kernel,
        grid=(T // tm,),
        in_specs=[row(1024), const(g), const(w)],
        out_specs=(row(1024), row(512), row(512), row(512), row(512), row(512),
                   row(LANE), row(LANE)),
        out_shape=out_shape,
        compiler_params=_params(("parallel",)),
        name="inproj",
    )(h, g, w)


def _mlstm_kernel(qk_ref, v_ref, o_ref, gi_ref, gf_ref, cw_ref, cb_ref, bi_ref, bf_ref,
                  gm_ref, tri_ref, out_ref, xpad, cext, mst, *, ts):
    s_idx = pl.program_id(1)

    @pl.when(s_idx == 0)
    def _():
        xpad[0:SUBLANE, :] = jnp.zeros((SUBLANE, 2 * D_MLSTM), F32)
        cext[...] = jnp.zeros(cext.shape, F32)
        mst[...] = jnp.full(mst.shape, M_INIT, F32)

    xpad[SUBLANE:SUBLANE + ts, :] = qk_ref[...]
    y = cb_ref[...]
    for tap in range(CONV_K):
        off = SUBLANE - (CONV_K - 1) + tap
        y = y + xpad[off:off + ts, :] * cw_ref[tap:tap + 1, :]
    xpad[0:SUBLANE, :] = xpad[ts:ts + SUBLANE, :]
    act = y * _sigmoid(y)
    q_all = act[:, :D_MLSTM].astype(BF16)
    kt_all = (act[:, D_MLSTM:] * (DH_MLSTM ** -0.5)).T

    row = lax.broadcasted_iota(jnp.int32, (CHUNK, CHUNK), 0)
    col = lax.broadcasted_iota(jnp.int32, (CHUNK, CHUNK), 1)
    causal = col <= row
    ones_blk = jnp.ones((CHUNK, DH_MLSTM), BF16)
    tri = tri_ref[...]

    for c in range(ts // CHUNK):
        r0 = c * CHUNK
        gi = gi_ref[r0:r0 + CHUNK, :] + bi_ref[...]
        gf = gf_ref[r0:r0 + CHUNK, :] + bf_ref[...]
        lf = jnp.minimum(gf, 0.0) - jnp.log(1.0 + jnp.exp(-jnp.abs(gf)))
        l1, l2, l3 = _split3(lf)
        bcum = _dot(tri, l1) + _dot(tri, l2) + _dot(tri, l3)
        a_all = gi - bcum
        a_t = a_all.T
        for h in range(H_MLSTM):
            hs = slice(h * DH_MLSTM, (h + 1) * DH_MLSTM)
            a_row = a_t[h:h + 1, :]
            m_prev = mst[h:h + 1, 0:1]
            mx = jnp.max(jnp.where(causal, a_row, -jnp.inf), axis=-1, keepdims=True)
            big_m = jnp.maximum(m_prev, mx)
            w = jnp.where(causal, jnp.exp(a_row - big_m), 0.0)
            a_inter = jnp.exp(m_prev - big_m)
            m_t = bcum[:, h:h + 1] + big_m
            qh = q_all[r0:r0 + CHUNK, hs]
            kt = kt_all[hs, r0:r0 + CHUNK]
            sb = (_dot(qh, kt.astype(BF16)) * w).astype(BF16)
            vext = jnp.concatenate([v_ref[r0:r0 + CHUNK, hs], ones_blk], axis=1)
            ce = cext[h]
            numext = a_inter * _dot(qh, ce.astype(BF16)) + _dot(sb, vext)
            num = numext[:, :DH_MLSTM]
            den = numext[:, DH_MLSTM:]
            hh = num / jnp.maximum(jnp.abs(den), jnp.exp(-m_t))
            ms = jnp.mean(hh * hh, axis=-1, keepdims=True)
            yh = hh * lax.rsqrt(ms + EPS) * gm_ref[:, hs]
            out_ref[r0:r0 + CHUNK, hs] = (_sigmoid(o_ref[r0:r0 + CHUNK, hs]) * yh).astype(BF16)
            m_last = big_m[CHUNK - 1:CHUNK, :]
            wkt = (kt * jnp.exp(a_row - m_last)).astype(BF16)
            cext[h] = jnp.exp(m_prev - m_last) * ce + _dot(wkt, vext)
            mst[h:h + 1, :] = jnp.broadcast_to(m_t[CHUNK - 1:CHUNK, :], (1, LANE))


def _mlstm(qk, v, o, gi, gf, conv_w, conv_b, bi, bf, gm, tri, B, S, ts):
    T = B * S
    nsb = S // ts
    row = lambda n: pl.BlockSpec((ts, n), lambda b, s: (b * nsb + s, 0))
    const = lambda a: pl.BlockSpec(a.shape, lambda b, s: (0, 0))
    return pl.pallas_call(
        functools.partial(_mlstm_kernel, ts=ts),
        grid=(B, nsb),
        in_specs=[row(1024), row(512), row(512), row(LANE), row(LANE),
                  const(conv_w), const(conv_b), const(bi), const(bf), const(gm), const(tri)],
        out_specs=row(512),
        out_shape=jax.ShapeDtypeStruct((T, D_MLSTM), BF16),
        scratch_shapes=[
            pltpu.VMEM((ts + SUBLANE, 2 * D_MLSTM), F32),
            pltpu.VMEM((H_MLSTM, DH_MLSTM, 2 * DH_MLSTM), F32),
            pltpu.VMEM((SUBLANE, LANE), F32),
        ],
        compiler_params=_params(("parallel", "arbitrary")),
        name="mlstm",
    )(qk, v, o, gi, gf, conv_w, conv_b, bi, bf, gm, tri)


def _sb_kernel(q_ref, k_ref, v_ref, g_ref, u_ref, out_ref, r_scr, acc_scr, *, tq, npb):
    qi = pl.program_id(2)
    lane = lax.broadcasted_iota(jnp.int32, (tq, LANE), 1)
    first = lane < DH_SB
    r_scr[...] = jnp.zeros(r_scr.shape, F32)
    acc_scr[...] = jnp.zeros(acc_scr.shape, F32)

    def head_q(p, e):
        q = q_ref[:, p * LANE:(p + 1) * LANE]
        zero = jnp.zeros_like(q)
        return jnp.where(first, q, zero) if e == 0 else jnp.where(first, zero, q)

    def sweep(g, diag):
        st = pl.multiple_of(g * tq, tq)
        heads = [(p, e) for p in range(npb) for e in range(2)]
        if diag:
            strict = (lax.broadcasted_iota(jnp.int32, (tq, tq), 1)
                      < lax.broadcasted_iota(jnp.int32, (tq, tq), 0))
        zs = [lax.dot_general(head_q(p, e), k_ref[pl.ds(st, tq), p * LANE:(p + 1) * LANE],
                              (((1,), (1,)), ((), ())), preferred_element_type=F32)
              for p, e in heads]
        lbs, xs = [], []
        for z in zs:
            sp = jnp.log(1.0 + jnp.exp(-jnp.abs(z)))
            lb = jnp.minimum(z, 0.0) - sp
            lr = lb - z
            if diag:
                lr = jnp.where(strict, lr, 0.0)
            lbs.append(lb)
            xs.append(_dot(lr.astype(BF16), u_ref[...]))
        for i, (p, e) in enumerate(heads):
            r = r_scr[i]
            a = jnp.exp(lbs[i] + xs[i][:, :tq] + jnp.concatenate([r] * (tq // LANE), axis=1))
            if diag:
                a = jnp.where(strict, a, 0.0)
            r_scr[i] = r + xs[i][:, tq:]
            acc_scr[i] += _dot(a.astype(BF16), v_ref[pl.ds(st, tq), p * LANE:(p + 1) * LANE])

    sweep(qi, True)

    def body(j, c):
        sweep(qi - 1 - j, False)
        return c

    lax.fori_loop(0, qi, body, 0)

    for p in range(npb):
        o = jnp.where(first, acc_scr[2 * p], acc_scr[2 * p + 1])
        sq = o * o
        s0 = jnp.sum(jnp.where(first, sq, 0.0), axis=-1, keepdims=True)
        s1 = jnp.sum(jnp.where(first, 0.0, sq), axis=-1, keepdims=True)
        ms = jnp.where(first, s0, s1) * (1.0 / DH_SB)
        ls = slice(p * LANE, (p + 1) * LANE)
        out_ref[:, ls] = (o * lax.rsqrt(ms + EPS) * g_ref[:, ls]).astype(BF16)


def _sb_attention(qs, ks, vs, g_sb, usuf, B, S, tq, npb):
    T = B * S
    nq = S // tq
    w = npb * LANE
    return pl.pallas_call(
        functools.partial(_sb_kernel, tq=tq, npb=npb),
        grid=(B, D_SB // w, nq),
        in_specs=[
            pl.BlockSpec((tq, w), lambda b, p, i: (b * nq + i, p)),
            pl.BlockSpec((S, w), lambda b, p, i: (b, p)),
            pl.BlockSpec((S, w), lambda b, p, i: (b, p)),
            pl.BlockSpec((1, w), lambda b, p, i: (0, p)),
            pl.BlockSpec(usuf.shape, lambda b, p, i: (0, 0)),
        ],
        out_specs=pl.BlockSpec((tq, w), lambda b, p, i: (b * nq + i, p)),
        out_shape=jax.ShapeDtypeStruct((T, D_SB), BF16),
        scratch_shapes=[pltpu.VMEM((2 * npb, tq, LANE), F32),
                        pltpu.VMEM((2 * npb, tq, LANE), F32)],
        compiler_params=_params(("parallel", "parallel", "arbitrary")),
        name="sb_attention",
    )(qs, ks, vs, g_sb, usuf)


def _outproj_kernel(hm_ref, hs_ref, w_ref, h_ref, g_ref, *rest, moe):
    if moe:
        wrh_ref, wrl_ref, hn_ref, yb_ref, yf_ref, ti_ref, tw_ref = rest
    else:
        hn_ref, yb_ref = rest
    hn = h_ref[...] + _dot(hm_ref[...], w_ref[0:D_MLSTM, :]) + _dot(hs_ref[...], w_ref[D_MLSTM:, :])
    hn_ref[...] = hn
    ms = jnp.mean(hn * hn, axis=-1, keepdims=True)
    y = hn * lax.rsqrt(ms + EPS) * g_ref[...]
    yb = y.astype(BF16)
    yb_ref[...] = yb
    if moe:
        yf_ref[...] = yb.astype(F32)
        yl = (y - yb.astype(F32)).astype(BF16)
        logits = _dot(yb, wrh_ref[...]) + _dot(yl, wrh_ref[...]) + _dot(yb, wrl_ref[...])
        lane = lax.broadcasted_iota(jnp.int32, logits.shape, 1)
        lanef = lane.astype(F32)
        lg = jnp.where(lane < N_EXPERTS, logits, -jnp.inf)
        m1 = jnp.max(lg, axis=-1, keepdims=True)
        i1 = jnp.min(jnp.where(lg == m1, lanef, float(LANE)), axis=-1, keepdims=True)
        lg2 = jnp.where(lanef == i1, -jnp.inf, lg)
        m2 = jnp.max(lg2, axis=-1, keepdims=True)
        i2 = jnp.min(jnp.where(lg2 == m2, lanef, float(LANE)), axis=-1, keepdims=True)
        t = jnp.exp(m2 - m1)
        w1 = 1.0 / (1.0 + t)
        w2 = t * w1
        ti_ref[...] = jnp.where(lane == 0, i1, jnp.where(lane == 1, i2, 0.0)).astype(jnp.int32)
        tw_ref[...] = jnp.where(lane == 0, w1, jnp.where(lane == 1, w2, 0.0))


def _outproj(hm, hs, w_out, h, g, tm, router=None):
    T = h.shape[0]
    moe = router is not None
    row = lambda n: pl.BlockSpec((tm, n), lambda i: (i, 0))
    const = lambda a: pl.BlockSpec(a.shape, lambda i: (0, 0))
    ins = [hm, hs, w_out, h, g]
    in_specs = [row(512), row(512), const(w_out), row(1024), const(g)]
    out_shape = [jax.ShapeDtypeStruct((T, D_MODEL), F32), jax.ShapeDtypeStruct((T, D_MODEL), BF16)]
    out_specs = [row(1024), row(1024)]
    if moe:
        ins += list(router)
        in_specs += [const(router[0]), const(router[1])]
        out_shape += [jax.ShapeDtypeStruct((T, D_MODEL), F32),
                      jax.ShapeDtypeStruct((T, LANE), jnp.int32),
                      jax.ShapeDtypeStruct((T, LANE), F32)]
        out_specs += [row(1024), row(LANE), row(LANE)]
    return pl.pallas_call(
        functools.partial(_outproj_kernel, moe=moe),
        grid=(T // tm,),
        in_specs=in_specs,
        out_specs=tuple(out_specs),
        out_shape=tuple(out_shape),
        compiler_params=_params(("parallel",)),
        name="outproj_moe" if moe else "outproj",
    )(*ins)


def _ffn_kernel(y_ref, h_ref, wg_ref, wu_ref, wd_ref, out_ref, *, fc):
    y = y_ref[...]
    out_ref[...] = h_ref[...]
    for c in range(wg_ref.shape[1] // fc):
        cs = slice(c * fc, (c + 1) * fc)
        g = _dot(y, wg_ref[:, cs])
        u = _dot(y, wu_ref[:, cs])
        a = (g * _sigmoid(g) * u).astype(BF16)
        out_ref[...] += _dot(a, wd_ref[cs, :])


def _ffn(y, h, wg, wu, wd, tm, fc):
    T = h.shape[0]
    row = lambda n: pl.BlockSpec((tm, n), lambda i: (i, 0))
    const = lambda a: pl.BlockSpec(a.shape, lambda i: (0, 0))
    return pl.pallas_call(
        functools.partial(_ffn_kernel, fc=fc),
        grid=(T // tm,),
        in_specs=[row(1024), row(1024), const(wg), const(wu), const(wd)],
        out_specs=row(1024),
        out_shape=jax.ShapeDtypeStruct((T, D_MODEL), F32),
        compiler_params=_params(("parallel",)),
        name="ffn",
    )(y, h, wg, wu, wd)


def _row_copy(src, dst, sem, s, d):
    return pltpu.make_async_copy(src.at[pl.ds(s, 1), :], dst.at[pl.ds(d, 1), :], sem)


def _expert_kernel(te_ref, nu_ref, idx_ref, nidx_ref, y_hbm, wg_ref, wu_ref, wd_ref, out_ref,
                   xbuf, xb, sem, *, tm):
    i = pl.program_id(0)
    f = pl.program_id(1)
    slot = lax.rem(i, 2)

    def gather(idx, s):
        def issue(r, c):
            _row_copy(y_hbm, xbuf.at[s], sem.at[s], idx[0, 0, r], r).start()
            return c
        lax.fori_loop(0, tm, issue, 0, unroll=8)

    @pl.when(f == 0)
    def _():
        @pl.when(i == 0)
        def _():
            gather(idx_ref, 0)

        def drain(r, c):
            _row_copy(y_hbm, xbuf.at[slot], sem.at[slot], 0, r).wait()
            return c
        lax.fori_loop(0, tm, drain, 0, unroll=8)
        xb[...] = xbuf[slot].astype(BF16)

        @pl.when(i + 1 < pl.num_programs(0))
        def _():
            gather(nidx_ref, 1 - slot)

        out_ref[...] = jnp.zeros(out_ref.shape, F32)

    @pl.when(i < nu_ref[0])
    def _():
        x = xb[...]
        g = _dot(x, wg_ref[...].astype(BF16))
        u = _dot(x, wu_ref[...].astype(BF16))
        a = (g * _sigmoid(g) * u).astype(BF16)
        out_ref[...] += _dot(a, wd_ref[...].astype(BF16))


def _experts(y, tok_of_slot, tile_expert, n_used, wg, wu, wd, layer, tm, tf):
    P = tok_of_slot.shape[0]
    n = P // tm
    F = wg.shape[3]
    idx = tok_of_slot.reshape(n, 1, tm)
    grid_spec = pltpu.PrefetchScalarGridSpec(
        num_scalar_prefetch=2,
        grid=(n, F // tf),
        in_specs=[
            pl.BlockSpec((1, 1, tm), lambda i, f, te, nu: (i, 0, 0), memory_space=pltpu.SMEM),
            pl.BlockSpec((1, 1, tm), lambda i, f, te, nu: (jnp.minimum(i + 1, n - 1), 0, 0),
                         memory_space=pltpu.SMEM),
            pl.BlockSpec(memory_space=pl.ANY),
            pl.BlockSpec((None, None, D_MODEL, tf), lambda i, f, te, nu: (layer, te[i], 0, f)),
            pl.BlockSpec((None, None, D_MODEL, tf), lambda i, f, te, nu: (layer, te[i], 0, f)),
            pl.BlockSpec((None, None, tf, D_MODEL), lambda i, f, te, nu: (layer, te[i], f, 0)),
        ],
        out_specs=pl.BlockSpec((tm, D_MODEL), lambda i, f, te, nu: (i, 0)),
        scratch_shapes=[pltpu.VMEM((2, tm, D_MODEL), F32), pltpu.VMEM((tm, D_MODEL), BF16),
                        pltpu.SemaphoreType.DMA((2,))],
    )
    return pl.pallas_call(
        functools.partial(_expert_kernel, tm=tm),
        grid_spec=grid_spec,
        out_shape=jax.ShapeDtypeStruct((P, D_MODEL), F32),
        compiler_params=_params(("arbitrary", "arbitrary")),
        name="moe_experts",
    )(tile_expert, n_used, idx, idx, y, wg, wu, wd)


def _combine_kernel(s0_ref, s1_ref, e_hbm, h_ref, tw_ref, out_ref, b0, b1, sem, *, rows):
    def issue(r, c):
        _row_copy(e_hbm, b0, sem.at[0], s0_ref[0, 0, r], r).start()
        _row_copy(e_hbm, b1, sem.at[1], s1_ref[0, 0, r], r).start()
        return c

    lax.fori_loop(0, rows, issue, 0, unroll=8)

    def drain(r, c):
        _row_copy(e_hbm, b0, sem.at[0], 0, r).wait()
        _row_copy(e_hbm, b1, sem.at[1], 0, r).wait()
        return c

    lax.fori_loop(0, rows, drain, 0, unroll=8)
    tw = tw_ref[...]
    out_ref[...] = h_ref[...] + tw[:, 0:1] * b0[...] + tw[:, 1:2] * b1[...]


def _combine(e_sorted, slot0, slot1, h, topw, rows):
    T = h.shape[0]
    n = T // rows
    smem = lambda: pl.BlockSpec((1, 1, rows), lambda i: (i, 0, 0), memory_space=pltpu.SMEM)
    return pl.pallas_call(
        functools.partial(_combine_kernel, rows=rows),
        grid=(n,),
        in_specs=[smem(), smem(), pl.BlockSpec(memory_space=pl.ANY),
                  pl.BlockSpec((rows, D_MODEL), lambda i: (i, 0)),
                  pl.BlockSpec((rows, LANE), lambda i: (i, 0))],
        out_specs=pl.BlockSpec((rows, D_MODEL), lambda i: (i, 0)),
        out_shape=jax.ShapeDtypeStruct((T, D_MODEL), F32),
        scratch_shapes=[pltpu.VMEM((rows, D_MODEL), F32), pltpu.VMEM((rows, D_MODEL), F32),
                        pltpu.SemaphoreType.DMA((2,))],
        compiler_params=_params(("arbitrary",)),
        name="moe_combine",
    )(slot0.reshape(n, 1, rows), slot1.reshape(n, 1, rows), e_sorted, h, topw)


def _route(topi, tm):
    T = topi.shape[0]
    e = topi[:, :2].reshape(-1)
    onehot = (e[:, None] == jnp.arange(N_EXPERTS, dtype=jnp.int32)[None, :]).astype(jnp.int32)
    csum = jnp.cumsum(onehot, axis=0)
    pos = jnp.sum((csum - onehot) * onehot, axis=1)
    counts = csum[-1]
    tiles = (counts + tm - 1) // tm
    tile_end = jnp.cumsum(tiles)
    offs = (tile_end - tiles) * tm
    slot = jnp.sum(onehot * offs[None, :], axis=1) + pos
    n_tiles = 2 * T // tm + N_EXPERTS
    n_used = tile_end[-1:].astype(jnp.int32)
    tile_ids = jnp.arange(n_tiles, dtype=jnp.int32)
    tile_expert = jnp.sum((tile_ids[:, None] >= tile_end[None, :]).astype(jnp.int32), axis=1)
    tile_expert = jnp.minimum(tile_expert, N_EXPERTS - 1).astype(jnp.int32)
    tok_of_slot = jnp.zeros((n_tiles * tm,), jnp.int32).at[slot].set(
        jnp.arange(2 * T, dtype=jnp.int32) // 2)
    slot2 = slot.reshape(T, 2)
    return tok_of_slot, tile_expert, n_used, slot2[:, 0], slot2[:, 1]


def _norm_kernel(x_ref, g_ref, o_ref):
    x = x_ref[...]
    ms = jnp.mean(x * x, axis=-1, keepdims=True)
    o_ref[...] = x * lax.rsqrt(ms + EPS) * g_ref[...]


def _final_norm(h, g, tm):
    T = h.shape[0]
    return pl.pallas_call(
        _norm_kernel,
        grid=(T // tm,),
        in_specs=[pl.BlockSpec((tm, D_MODEL), lambda i: (i, 0)),
                  pl.BlockSpec((1, D_MODEL), lambda i: (0, 0))],
        out_specs=pl.BlockSpec((tm, D_MODEL), lambda i: (i, 0)),
        out_shape=jax.ShapeDtypeStruct((T, D_MODEL), F32),
        compiler_params=_params(("parallel",)),
        name="final_norm",
    )(h, g)


def _pad_cols(w, n):
    return jnp.pad(w, ((0, 0), (0, n - w.shape[1])))


def _prep_w_in(w):
    gates0 = 4 * D_MLSTM
    gi = _pad_cols(w[:, gates0:gates0 + H_MLSTM], LANE)
    gf = _pad_cols(w[:, gates0 + H_MLSTM:gates0 + 2 * H_MLSTM], LANE)
    rest = w[:, gates0 + 2 * H_MLSTM:]
    return jnp.concatenate([w[:, :gates0], rest, gi, gf], axis=1).astype(BF16)


def kernel(x, norm_mix_g, w_in, b_igate, b_fgate, conv_w, conv_b, g_mlstm, g_sb, w_out,
           norm_ffn_g, ffn_w_gate, ffn_w_up, ffn_w_down, w_router, moe_w_gate, moe_w_up,
           moe_w_down, norm_final_g):
    B, S, D = x.shape
    T = B * S
    depth = w_in.shape[0]
    tm = min(512, T)
    ts = min(256, S)
    tq = min(256, S)
    tm_moe = min(1024, T)
    rows = min(512, T)

    ii = jnp.arange(CHUNK)
    tri = (ii[None, :] <= ii[:, None]).astype(BF16)
    jj = jnp.arange(tq)
    upper = (jj[:, None] > jj[None, :]).astype(BF16)
    usuf = jnp.concatenate([upper, jnp.ones((tq, LANE), BF16)], axis=1)

    f_dense = ffn_w_gate.shape[2]
    f_pad = -(-f_dense // 256) * 256

    h = x.reshape(T, D)
    for layer in range(depth):
        w1 = _prep_w_in(w_in[layer])
        qk, v_m, o_m, q_s, k_s, v_s, gi, gf = _inproj(h, norm_mix_g[layer][None, :], w1, tm)
        bi = _pad_cols(b_igate[layer][None, :], LANE)
        bf = _pad_cols(b_fgate[layer][None, :], LANE)
        h_m = _mlstm(qk, v_m, o_m, gi, gf, conv_w[layer], conv_b[layer][None, :], bi, bf,
                     g_mlstm[layer][None, :], tri, B, S, ts)
        h_s = _sb_attention(q_s, k_s, v_s, g_sb[layer][None, :], usuf, B, S, tq, 4)
        wo = w_out[layer].astype(BF16)
        gffn = norm_ffn_g[layer][None, :]
        j = layer // 2
        if layer % 2 == 0:
            h, yb = _outproj(h_m, h_s, wo, h, gffn, tm)
            wg = _pad_cols(ffn_w_gate[j], f_pad).astype(BF16)
            wu = _pad_cols(ffn_w_up[j], f_pad).astype(BF16)
            wd = jnp.pad(ffn_w_down[j], ((0, f_pad - f_dense), (0, 0))).astype(BF16)
            h = _ffn(yb, h, wg, wu, wd, tm, 256)
        else:
            wr = _pad_cols(w_router[j], LANE)
            wrh = wr.astype(BF16)
            wrl = (wr - wrh.astype(F32)).astype(BF16)
            h, yb, yf, topi, topw = _outproj(h_m, h_s, wo, h, gffn, tm, router=(wrh, wrl))
            tok_of_slot, tile_expert, n_used, slot0, slot1 = _route(topi, tm_moe)
            e_sorted = _experts(yf, tok_of_slot, tile_expert, n_used, moe_w_gate, moe_w_up,
                                moe_w_down, j, tm_moe, 512)
            h = _combine(e_sorted, slot0, slot1, h, topw, rows)
    out = _final_norm(h, norm_final_g[None, :], tm)
    return out.reshape(B, S, D)
```

```python
import functools

import jax
import jax.numpy as jnp
from jax import lax
from jax.experimental import pallas as pl
from jax.experimental.pallas import tpu as pltpu

F32 = jnp.float32
BF16 = jnp.bfloat16

D_MODEL = 1024
D_MLSTM = 512
H_MLSTM = 4
DH_MLSTM = 128
D_SB = 512
H_SB = 8
DH_SB = 64
CONV_K = 4
N_EXPERTS = 8
EPS = 1e-6
M_INIT = -1e30
SKEW = 1

LANE = 128
SUBLANE = 8
CHUNK = 128
VMEM_LIMIT = 52 * 1024 * 1024

C_QK, C_V, C_O, C_QS, C_KS, C_VS, C_GI, C_GF, C_END = (
    0, 1024, 1536, 2048, 2560, 3072, 3584, 3712, 3840)


def _params(sem, **kw):
    return pltpu.CompilerParams(dimension_semantics=sem, vmem_limit_bytes=VMEM_LIMIT, **kw)


def _sigmoid(x):
    return 1.0 / (1.0 + jnp.exp(-x))


def _split3(x):
    a = x.astype(BF16)
    r = x - a.astype(F32)
    b = r.astype(BF16)
    c = (r - b.astype(F32)).astype(BF16)
    return a, b, c


def _dot(a, b):
    return jnp.dot(a, b, preferred_element_type=F32)


def _inproj_kernel(x_ref, g_ref, w_ref, qk_ref, v_ref, o_ref, qs_ref, ks_ref, vs_ref,
                   gi_ref, gf_ref):
    x = x_ref[...]
    ms = jnp.mean(x * x, axis=-1, keepdims=True)
    xn = (x * lax.rsqrt(ms + EPS) * g_ref[...]).astype(BF16)

    def mm(lo, hi):
        return _dot(xn, w_ref[:, lo:hi])

    qk_ref[...] = mm(C_QK, C_V)
    v_ref[...] = mm(C_V, C_O).astype(BF16)
    o_ref[...] = mm(C_O, C_QS)
    qs_ref[...] = (mm(C_QS, C_KS) * (DH_SB ** -0.5)).astype(BF16)
    ks_ref[...] = mm(C_KS, C_VS).astype(BF16)
    vs_ref[...] = mm(C_VS, C_GI).astype(BF16)
    gi_ref[...] = mm(C_GI, C_GF)
    gf_ref[...] = mm(C_GF, C_END)


def _inproj(h, g, w, tm):
    T = h.shape[0]
    row = lambda n: pl.BlockSpec((tm, n), lambda i: (i, 0))
    const = lambda a: pl.BlockSpec(a.shape, lambda i: (0, 0))
    out_shape = (
        jax.ShapeDtypeStruct((T, 1024), F32),
        jax.ShapeDtypeStruct((T, 512), BF16),
        jax.ShapeDtypeStruct((T, 512), F32),
        jax.ShapeDtypeStruct((T, 512), BF16),
        jax.ShapeDtypeStruct((T, 512), BF16),
        jax.ShapeDtypeStruct((T, 512), BF16),
        jax.ShapeDtypeStruct((T, LANE), F32),
        jax.ShapeDtypeStruct((T, LANE), F32),
    )
    return pl.pallas_call(
        _inproj_kernel,
        grid=(T // tm,),
        in_specs=[row(1024), const(g), const(w)],
        out_specs=(row(1024), row(512), row(512), row(512), row(512), row(512),
                   row(LANE), row(LANE)),
        out_shape=out_shape,
        compiler_params=_params(("parallel",)),
        name="inproj",
    )(h, g, w)


def _mlstm_kernel(qk_ref, v_ref, o_ref, gi_ref, gf_ref, cw_ref, cb_ref, bi_ref, bf_ref,
                  gm_ref, tri_ref, out_ref, xpad, cext, mst, *, ts):
    s_idx = pl.program_id(1)

    @pl.when(s_idx == 0)
    def _():
        xpad[0:SUBLANE, :] = jnp.zeros((SUBLANE, 2 * D_MLSTM), F32)
        cext[...] = jnp.zeros(cext.shape, F32)
        mst[...] = jnp.full(mst.shape, M_INIT, F32)

    xpad[SUBLANE:SUBLANE + ts, :] = qk_ref[...]
    y = cb_ref[...]
    for tap in range(CONV_K):
        off = SUBLANE - (CONV_K - 1) + tap
        y = y + xpad[off:off + ts, :] * cw_ref[tap:tap + 1, :]
    xpad[0:SUBLANE, :] = xpad[ts:ts + SUBLANE, :]
    act = y * _sigmoid(y)
    q_all = act[:, :D_MLSTM].astype(BF16)
    kt_all = (act[:, D_MLSTM:] * (DH_MLSTM ** -0.5)).T

    row = lax.broadcasted_iota(jnp.int32, (CHUNK, CHUNK), 0)
    col = lax.broadcasted_iota(jnp.int32, (CHUNK, CHUNK), 1)
    causal = col <= row
    ones_blk = jnp.ones((CHUNK, DH_MLSTM), BF16)
    tri = tri_ref[...]

    for c in range(ts // CHUNK):
        r0 = c * CHUNK
        gi = gi_ref[r0:r0 + CHUNK, :] + bi_ref[...]
        gf = gf_ref[r0:r0 + CHUNK, :] + bf_ref[...]
        lf = jnp.minimum(gf, 0.0) - jnp.log(1.0 + jnp.exp(-jnp.abs(gf)))
        l1, l2, l3 = _split3(lf)
        bcum = _dot(tri, l1) + _dot(tri, l2) + _dot(tri, l3)
        a_all = gi - bcum
        a_t = a_all.T
        for h in range(H_MLSTM):
            hs = slice(h * DH_MLSTM, (h + 1) * DH_MLSTM)
            a_row = a_t[h:h + 1, :]
            m_prev = mst[h:h + 1, 0:1]
            mx = jnp.max(jnp.where(causal, a_row, -jnp.inf), axis=-1, keepdims=True)
            big_m = jnp.maximum(m_prev, mx)
            w = jnp.where(causal, jnp.exp(a_row - big_m), 0.0)
            a_inter = jnp.exp(m_prev - big_m)
            m_t = bcum[:, h:h + 1] + big_m
            qh = q_all[r0:r0 + CHUNK, hs]
            kt = kt_all[hs, r0:r0 + CHUNK]
            sb = (_dot(qh, kt.astype(BF16)) * w).astype(BF16)
            vext = jnp.concatenate([v_ref[r0:r0 + CHUNK, hs], ones_blk], axis=1)
            ce = cext[h]
            numext = a_inter * _dot(qh, ce.astype(BF16)) + _dot(sb, vext)
            num = numext[:, :DH_MLSTM]
            den = numext[:, DH_MLSTM:]
            hh = num / jnp.maximum(jnp.abs(den), jnp.exp(-m_t))
            ms = jnp.mean(hh * hh, axis=-1, keepdims=True)
            yh = hh * lax.rsqrt(ms + EPS) * gm_ref[:, hs]
            out_ref[r0:r0 + CHUNK, hs] = (_sigmoid(o_ref[r0:r0 + CHUNK, hs]) * yh).astype(BF16)
            m_last = big_m[CHUNK - 1:CHUNK, :]
            wkt = (kt * jnp.exp(a_row - m_last)).astype(BF16)
            cext[h] = jnp.exp(m_prev - m_last) * ce + _dot(wkt, vext)
            mst[h:h + 1, :] = jnp.broadcast_to(m_t[CHUNK - 1:CHUNK, :], (1, LANE))


def _mlstm(qk, v, o, gi, gf, conv_w, conv_b, bi, bf, gm, tri, B, S, ts):
    T = B * S
    nsb = S // ts
    row = lambda n: pl.BlockSpec((ts, n), lambda b, s: (b * nsb + s, 0))
    const = lambda a: pl.BlockSpec(a.shape, lambda b, s: (0, 0))
    return pl.pallas_call(
        functools.partial(_mlstm_kernel, ts=ts),
        grid=(B, nsb),
        in_specs=[row(1024), row(512), row(512), row(LANE), row(LANE),
                  const(conv_w), const(conv_b), const(bi), const(bf), const(gm), const(tri)],
        out_specs=row(512),
        out_shape=jax.ShapeDtypeStruct((T, D_MLSTM), BF16),
        scratch_shapes=[
            pltpu.VMEM((ts + SUBLANE, 2 * D_MLSTM), F32),
            pltpu.VMEM((H_MLSTM, DH_MLSTM, 2 * DH_MLSTM), F32),
            pltpu.VMEM((SUBLANE, LANE), F32),
        ],
        compiler_params=_params(("parallel", "arbitrary")),
        name="mlstm",
    )(qk, v, o, gi, gf, conv_w, conv_b, bi, bf, gm, tri)


def _sb_kernel(q_ref, k_ref, v_ref, g_ref, u_ref, out_ref, r_scr, acc_scr, *, tq, npb):
    qi = pl.program_id(2)
    lane = lax.broadcasted_iota(jnp.int32, (tq, LANE), 1)
    first = lane < DH_SB
    r_scr[...] = jnp.zeros(r_scr.shape, F32)
    acc_scr[...] = jnp.zeros(acc_scr.shape, F32)

    def head_q(p, e):
        q = q_ref[:, p * LANE:(p + 1) * LANE]
        zero = jnp.zeros_like(q)
        return jnp.where(first, q, zero) if e == 0 else jnp.where(first, zero, q)

    def sweep(g, diag):
        st = pl.multiple_of(g * tq, tq)
        heads = [(p, e) for p in range(npb) for e in range(2)]
        if diag:
            strict = (lax.broadcasted_iota(jnp.int32, (tq, tq), 1)
                      < lax.broadcasted_iota(jnp.int32, (tq, tq), 0))
        n = len(heads)
        zs, lbs, xs = [None] * n, [None] * n, [None] * n

        def scores(i):
            p, e = heads[i]
            zs[i] = lax.dot_general(head_q(p, e), k_ref[pl.ds(st, tq), p * LANE:(p + 1) * LANE],
                                    (((1,), (1,)), ((), ())), preferred_element_type=F32)

        def logs(i):
            z = zs[i]
            sp = jnp.log(1.0 + jnp.exp(-jnp.abs(z)))
            lb = jnp.minimum(z, 0.0) - sp
            lr = lb - z
            if diag:
                lr = jnp.where(strict, lr, 0.0)
            lbs[i] = lb
            xs[i] = _dot(lr.astype(BF16), u_ref[...])

        def weights(i):
            p, e = heads[i]
            r = r_scr[i]
            a = jnp.exp(lbs[i] + xs[i][:, :tq] + jnp.concatenate([r] * (tq // LANE), axis=1))
            if diag:
                a = jnp.where(strict, a, 0.0)
            r_scr[i] = r + xs[i][:, tq:]
            acc_scr[i] += _dot(a.astype(BF16), v_ref[pl.ds(st, tq), p * LANE:(p + 1) * LANE])

        for s in range(n + 2 * SKEW):
            if s < n:
                scores(s)
            if 0 <= s - SKEW < n:
                logs(s - SKEW)
            if 0 <= s - 2 * SKEW < n:
                weights(s - 2 * SKEW)

    sweep(qi, True)

    def body(j, c):
        sweep(qi - 1 - j, False)
        return c

    lax.fori_loop(0, qi, body, 0)

    for p in range(npb):
        o = jnp.where(first, acc_scr[2 * p], acc_scr[2 * p + 1])
        sq = o * o
        s0 = jnp.sum(jnp.where(first, sq, 0.0), axis=-1, keepdims=True)
        s1 = jnp.sum(jnp.where(first, 0.0, sq), axis=-1, keepdims=True)
        ms = jnp.where(first, s0, s1) * (1.0 / DH_SB)
        ls = slice(p * LANE, (p + 1) * LANE)
        out_ref[:, ls] = (o * lax.rsqrt(ms + EPS) * g_ref[:, ls]).astype(BF16)


def _sb_attention(qs, ks, vs, g_sb, usuf, B, S, tq, npb):
    T = B * S
    nq = S // tq
    w = npb * LANE
    return pl.pallas_call(
        functools.partial(_sb_kernel, tq=tq, npb=npb),
        grid=(B, D_SB // w, nq),
        in_specs=[
            pl.BlockSpec((tq, w), lambda b, p, i: (b * nq + i, p)),
            pl.BlockSpec((S, w), lambda b, p, i: (b, p)),
            pl.BlockSpec((S, w), lambda b, p, i: (b, p)),
            pl.BlockSpec((1, w), lambda b, p, i: (0, p)),
            pl.BlockSpec(usuf.shape, lambda b, p, i: (0, 0)),
        ],
        out_specs=pl.BlockSpec((tq, w), lambda b, p, i: (b * nq + i, p)),
        out_shape=jax.ShapeDtypeStruct((T, D_SB), BF16),
        scratch_shapes=[pltpu.VMEM((2 * npb, tq, LANE), F32),
                        pltpu.VMEM((2 * npb, tq, LANE), F32)],
        compiler_params=_params(("parallel", "parallel", "arbitrary")),
        name="sb_attention",
    )(qs, ks, vs, g_sb, usuf)


def _outproj_kernel(hm_ref, hs_ref, w_ref, h_ref, g_ref, *rest, moe):
    if moe:
        wrh_ref, wrl_ref, hn_ref, yb_ref, yf_ref, ti_ref, tw_ref = rest
    else:
        hn_ref, yb_ref = rest
    hn = h_ref[...] + _dot(hm_ref[...], w_ref[0:D_MLSTM, :]) + _dot(hs_ref[...], w_ref[D_MLSTM:, :])
    hn_ref[...] = hn
    ms = jnp.mean(hn * hn, axis=-1, keepdims=True)
    y = hn * lax.rsqrt(ms + EPS) * g_ref[...]
    yb = y.astype(BF16)
    yb_ref[...] = yb
    if moe:
        yf_ref[...] = yb.astype(F32)
        yl = (y - yb.astype(F32)).astype(BF16)
        logits = _dot(yb, wrh_ref[...]) + _dot(yl, wrh_ref[...]) + _dot(yb, wrl_ref[...])
        lane = lax.broadcasted_iota(jnp.int32, logits.shape, 1)
        lanef = lane.astype(F32)
        lg = jnp.where(lane < N_EXPERTS, logits, -jnp.inf)
        m1 = jnp.max(lg, axis=-1, keepdims=True)
        i1 = jnp.min(jnp.where(lg == m1, lanef, float(LANE)), axis=-1, keepdims=True)
        lg2 = jnp.where(lanef == i1, -jnp.inf, lg)
        m2 = jnp.max(lg2, axis=-1, keepdims=True)
        i2 = jnp.min(jnp.where(lg2 == m2, lanef, float(LANE)), axis=-1, keepdims=True)
        t = jnp.exp(m2 - m1)
        w1 = 1.0 / (1.0 + t)
        w2 = t * w1
        ti_ref[...] = jnp.where(lane == 0, i1, jnp.where(lane == 1, i2, 0.0)).astype(jnp.int32)
        tw_ref[...] = jnp.where(lane == 0, w1, jnp.where(lane == 1, w2, 0.0))


def _outproj(hm, hs, w_out, h, g, tm, router=None):
    T = h.shape[0]
    moe = router is not None
    row = lambda n: pl.BlockSpec((tm, n), lambda i: (i, 0))
    const = lambda a: pl.BlockSpec(a.shape, lambda i: (0, 0))
    ins = [hm, hs, w_out, h, g]
    in_specs = [row(512), row(512), const(w_out), row(1024), const(g)]
    out_shape = [jax.ShapeDtypeStruct((T, D_MODEL), F32), jax.ShapeDtypeStruct((T, D_MODEL), BF16)]
    out_specs = [row(1024), row(1024)]
    if moe:
        ins += list(router)
        in_specs += [const(router[0]), const(router[1])]
        out_shape += [jax.ShapeDtypeStruct((T, D_MODEL), F32),
                      jax.ShapeDtypeStruct((T, LANE), jnp.int32),
                      jax.ShapeDtypeStruct((T, LANE), F32)]
        out_specs += [row(1024), row(LANE), row(LANE)]
    return pl.pallas_call(
        functools.partial(_outproj_kernel, moe=moe),
        grid=(T // tm,),
        in_specs=in_specs,
        out_specs=tuple(out_specs),
        out_shape=tuple(out_shape),
        compiler_params=_params(("parallel",)),
        name="outproj_moe" if moe else "outproj",
    )(*ins)


def _ffn_kernel(y_ref, h_ref, wg_ref, wu_ref, wd_ref, out_ref, *, fc):
    y = y_ref[...]
    out_ref[...] = h_ref[...]
    for c in range(wg_ref.shape[1] // fc):
        cs = slice(c * fc, (c + 1) * fc)
        g = _dot(y, wg_ref[:, cs])
        u = _dot(y, wu_ref[:, cs])
        a = (g * _sigmoid(g) * u).astype(BF16)
        out_ref[...] += _dot(a, wd_ref[cs, :])


def _ffn(y, h, wg, wu, wd, tm, fc):
    T = h.shape[0]
    row = lambda n: pl.BlockSpec((tm, n), lambda i: (i, 0))
    const = lambda a: pl.BlockSpec(a.shape, lambda i: (0, 0))
    return pl.pallas_call(
        functools.partial(_ffn_kernel, fc=fc),
        grid=(T // tm,),
        in_specs=[row(1024), row(1024), const(wg), const(wu), const(wd)],
        out_specs=row(1024),
        out_shape=jax.ShapeDtypeStruct((T, D_MODEL), F32),
        compiler_params=_params(("parallel",)),
        name="ffn",
    )(y, h, wg, wu, wd)


def _row_copy(src, dst, sem, s, d):
    return pltpu.make_async_copy(src.at[pl.ds(s, 1), :], dst.at[pl.ds(d, 1), :], sem)


def _expert_kernel(te_ref, nu_ref, idx_ref, nidx_ref, y_hbm, wg_ref, wu_ref, wd_ref, out_ref,
                   xbuf, xb, sem, *, tm, chunk):
    i = pl.program_id(0)
    f = pl.program_id(1)
    slot = lax.rem(i, 2)

    def gather(idx, s, lo, groups):
        def issue(k, c):
            base = pl.multiple_of(lo + k * SUBLANE, SUBLANE)
            for u in range(SUBLANE):
                _row_copy(y_hbm, xbuf.at[s], sem.at[s], idx[0, 0, base + u], base + u).start()
            return c
        lax.fori_loop(0, groups, issue, 0)

    @pl.when(f == 0)
    def _():
        @pl.when(i == 0)
        def _():
            gather(idx_ref, 0, 0, tm // SUBLANE)

        def drain(r, c):
            _row_copy(y_hbm, xbuf.at[slot], sem.at[slot], 0, r).wait()
            return c
        lax.fori_loop(0, tm, drain, 0, unroll=8)
        xb[...] = xbuf[slot].astype(BF16)
        out_ref[...] = jnp.zeros(out_ref.shape, F32)

    @pl.when(i + 1 < pl.num_programs(0))
    def _():
        lo = f * chunk
        gather(nidx_ref, 1 - slot, lo, (jnp.minimum(lo + chunk, tm) - lo) // SUBLANE)

    @pl.when(i < nu_ref[0])
    def _():
        x = xb[...]
        g = _dot(x, wg_ref[...].astype(BF16))
        u = _dot(x, wu_ref[...].astype(BF16))
        a = (g * _sigmoid(g) * u).astype(BF16)
        out_ref[...] += _dot(a, wd_ref[...].astype(BF16))


def _experts(y, tok_of_slot, tile_expert, n_used, wg, wu, wd, layer, tm, tf):
    P = tok_of_slot.shape[0]
    n = P // tm
    F = wg.shape[3]
    idx = tok_of_slot.reshape(n, 1, tm)
    rows_per_step = -(-tm // (F // tf))
    chunk = -(-rows_per_step // SUBLANE) * SUBLANE
    grid_spec = pltpu.PrefetchScalarGridSpec(
        num_scalar_prefetch=2,
        grid=(n, F // tf),
        in_specs=[
            pl.BlockSpec((1, 1, tm), lambda i, f, te, nu: (i, 0, 0), memory_space=pltpu.SMEM),
            pl.BlockSpec((1, 1, tm), lambda i, f, te, nu: (jnp.minimum(i + 1, n - 1), 0, 0),
                         memory_space=pltpu.SMEM),
            pl.BlockSpec(memory_space=pl.ANY),
            pl.BlockSpec((None, None, D_MODEL, tf), lambda i, f, te, nu: (layer, te[i], 0, f)),
            pl.BlockSpec((None, None, D_MODEL, tf), lambda i, f, te, nu: (layer, te[i], 0, f)),
            pl.BlockSpec((None, None, tf, D_MODEL), lambda i, f, te, nu: (layer, te[i], f, 0)),
        ],
        out_specs=pl.BlockSpec((tm, D_MODEL), lambda i, f, te, nu: (i, 0)),
        scratch_shapes=[pltpu.VMEM((2, tm, D_MODEL), F32), pltpu.VMEM((tm, D_MODEL), BF16),
                        pltpu.SemaphoreType.DMA((2,))],
    )
    return pl.pallas_call(
        functools.partial(_expert_kernel, tm=tm, chunk=chunk),
        grid_spec=grid_spec,
        out_shape=jax.ShapeDtypeStruct((P, D_MODEL), F32),
        compiler_params=_params(("arbitrary", "arbitrary")),
        name="moe_experts",
    )(tile_expert, n_used, idx, idx, y, wg, wu, wd)


def _combine_kernel(s0_ref, s1_ref, e_hbm, h_ref, tw_ref, out_ref, b0, b1, sem, *, rows):
    def issue(r, c):
        _row_copy(e_hbm, b0, sem.at[0], s0_ref[0, 0, r], r).start()
        _row_copy(e_hbm, b1, sem.at[1], s1_ref[0, 0, r], r).start()
        return c

    lax.fori_loop(0, rows, issue, 0, unroll=8)

    def drain(r, c):
        _row_copy(e_hbm, b0, sem.at[0], 0, r).wait()
        _row_copy(e_hbm, b1, sem.at[1], 0, r).wait()
        return c

    lax.fori_loop(0, rows, drain, 0, unroll=8)
    tw = tw_ref[...]
    out_ref[...] = h_ref[...] + tw[:, 0:1] * b0[...] + tw[:, 1:2] * b1[...]


def _combine(e_sorted, slot0, slot1, h, topw, rows):
    T = h.shape[0]
    n = T // rows
    smem = lambda: pl.BlockSpec((1, 1, rows), lambda i: (i, 0, 0), memory_space=pltpu.SMEM)
    return pl.pallas_call(
        functools.partial(_combine_kernel, rows=rows),
        grid=(n,),
        in_specs=[smem(), smem(), pl.BlockSpec(memory_space=pl.ANY),
                  pl.BlockSpec((rows, D_MODEL), lambda i: (i, 0)),
                  pl.BlockSpec((rows, LANE), lambda i: (i, 0))],
        out_specs=pl.BlockSpec((rows, D_MODEL), lambda i: (i, 0)),
        out_shape=jax.ShapeDtypeStruct((T, D_MODEL), F32),
        scratch_shapes=[pltpu.VMEM((rows, D_MODEL), F32), pltpu.VMEM((rows, D_MODEL), F32),
                        pltpu.SemaphoreType.DMA((2,))],
        compiler_params=_params(("arbitrary",)),
        name="moe_combine",
    )(slot0.reshape(n, 1, rows), slot1.reshape(n, 1, rows), e_sorted, h, topw)


def _route(topi, tm):
    T = topi.shape[0]
    e = topi[:, :2].reshape(-1)
    onehot = (e[:, None] == jnp.arange(N_EXPERTS, dtype=jnp.int32)[None, :]).astype(jnp.int32)
    csum = jnp.cumsum(onehot, axis=0)
    pos = jnp.sum((csum - onehot) * onehot, axis=1)
    counts = csum[-1]
    tiles = (counts + tm - 1) // tm
    tile_end = jnp.cumsum(tiles)
    offs = (tile_end - tiles) * tm
    slot = jnp.sum(onehot * offs[None, :], axis=1) + pos
    n_tiles = 2 * T // tm + N_EXPERTS
    n_used = tile_end[-1:].astype(jnp.int32)
    tile_ids = jnp.arange(n_tiles, dtype=jnp.int32)
    tile_expert = jnp.sum((tile_ids[:, None] >= tile_end[None, :]).astype(jnp.int32), axis=1)
    tile_expert = jnp.minimum(tile_expert, N_EXPERTS - 1).astype(jnp.int32)
    tok_of_slot = jnp.zeros((n_tiles * tm,), jnp.int32).at[slot].set(
        jnp.arange(2 * T, dtype=jnp.int32) // 2)
    slot2 = slot.reshape(T, 2)
    return tok_of_slot, tile_expert, n_used, slot2[:, 0], slot2[:, 1]


def _norm_kernel(x_ref, g_ref, o_ref):
    x = x_ref[...]
    ms = jnp.mean(x * x, axis=-1, keepdims=True)
    o_ref[...] = x * lax.rsqrt(ms + EPS) * g_ref[...]


def _final_norm(h, g, tm):
    T = h.shape[0]
    return pl.pallas_call(
        _norm_kernel,
        grid=(T // tm,),
        in_specs=[pl.BlockSpec((tm, D_MODEL), lambda i: (i, 0)),
                  pl.BlockSpec((1, D_MODEL), lambda i: (0, 0))],
        out_specs=pl.BlockSpec((tm, D_MODEL), lambda i: (i, 0)),
        out_shape=jax.ShapeDtypeStruct((T, D_MODEL), F32),
        compiler_params=_params(("parallel",)),
        name="final_norm",
    )(h, g)


def _pad_cols(w, n):
    return jnp.pad(w, ((0, 0), (0, n - w.shape[1])))


def _prep_w_in(w):
    gates0 = 4 * D_MLSTM
    gi = _pad_cols(w[:, gates0:gates0 + H_MLSTM], LANE)
    gf = _pad_cols(w[:, gates0 + H_MLSTM:gates0 + 2 * H_MLSTM], LANE)
    rest = w[:, gates0 + 2 * H_MLSTM:]
    return jnp.concatenate([w[:, :gates0], rest, gi, gf], axis=1).astype(BF16)


def kernel(x, norm_mix_g, w_in, b_igate, b_fgate, conv_w, conv_b, g_mlstm, g_sb, w_out,
           norm_ffn_g, ffn_w_gate, ffn_w_up, ffn_w_down, w_router, moe_w_gate, moe_w_up,
           moe_w_down, norm_final_g):
    B, S, D = x.shape
    T = B * S
    depth = w_in.shape[0]
    tm = min(512, T)
    ts = min(256, S)
    tq = min(256, S)
    tm_moe = min(1024, T)
    rows = min(512, T)

    ii = jnp.arange(CHUNK)
    tri = (ii[None, :] <= ii[:, None]).astype(BF16)
    jj = jnp.arange(tq)
    upper = (jj[:, None] > jj[None, :]).astype(BF16)
    usuf = jnp.concatenate([upper, jnp.ones((tq, LANE), BF16)], axis=1)

    f_dense = ffn_w_gate.shape[2]
    f_pad = -(-f_dense // 256) * 256

    h = x.reshape(T, D)
    for layer in range(depth):
        w1 = _prep_w_in(w_in[layer])
        qk, v_m, o_m, q_s, k_s, v_s, gi, gf = _inproj(h, norm_mix_g[layer][None, :], w1, tm)
        bi = _pad_cols(b_igate[layer][None, :], LANE)
        bf = _pad_cols(b_fgate[layer][None, :], LANE)
        h_m = _mlstm(qk, v_m, o_m, gi, gf, conv_w[layer], conv_b[layer][None, :], bi, bf,
                     g_mlstm[layer][None, :], tri, B, S, ts)
        h_s = _sb_attention(q_s, k_s, v_s, g_sb[layer][None, :], usuf, B, S, tq, 4)
        wo = w_out[layer].astype(BF16)
        gffn = norm_ffn_g[layer][None, :]
        j = layer // 2
        if layer % 2 == 0:
            h, yb = _outproj(h_m, h_s, wo, h, gffn, tm)
            wg = _pad_cols(ffn_w_gate[j], f_pad).astype(BF16)
            wu = _pad_cols(ffn_w_up[j], f_pad).astype(BF16)
            wd = jnp.pad(ffn_w_down[j], ((0, f_pad - f_dense), (0, 0))).astype(BF16)
            h = _ffn(yb, h, wg, wu, wd, tm, 256)
        else:
            wr = _pad_cols(w_router[j], LANE)
            wrh = wr.astype(BF16)
            wrl = (wr - wrh.astype(F32)).astype(BF16)
            h, yb, yf, topi, topw = _outproj(h_m, h_s, wo, h, gffn, tm, router=(wrh, wrl))
            tok_of_slot, tile_expert, n_used, slot0, slot1 = _route(topi, tm_moe)
            e_sorted = _experts(yf, tok_of_slot, tile_expert, n_used, moe_w_gate, moe_w_up,
                                moe_w_down, j, tm_moe, 512)
            h = _combine(e_sorted, slot0, slot1, h, topw, rows)
    out = _final_norm(h, norm_final_g[None, :], tm)
    return out.reshape(B, S, D)
```

```python
import functools

import jax
import jax.numpy as jnp
from jax import lax
from jax.experimental import pallas as pl
from jax.experimental.pallas import tpu as pltpu

F32 = jnp.float32
BF16 = jnp.bfloat16

D_MODEL = 1024
D_MLSTM = 512
H_MLSTM = 4
DH_MLSTM = 128
D_SB = 512
H_SB = 8
DH_SB = 64
CONV_K = 4
N_EXPERTS = 8
EPS = 1e-6
M_INIT = -1e30
SKEW = 1

LANE = 128
SUBLANE = 8
assert D_MODEL == SUBLANE * LANE
CHUNK = 128
VMEM_LIMIT = 52 * 1024 * 1024

C_QK, C_V, C_O, C_QS, C_KS, C_VS, C_GI, C_GF, C_END = (
    0, 1024, 1536, 2048, 2560, 3072, 3584, 3712, 3840)


def _params(sem, **kw):
    return pltpu.CompilerParams(dimension_semantics=sem, vmem_limit_bytes=VMEM_LIMIT, **kw)


def _sigmoid(x):
    return 1.0 / (1.0 + jnp.exp(-x))


def _split3(x):
    a = x.astype(BF16)
    r = x - a.astype(F32)
    b = r.astype(BF16)
    c = (r - b.astype(F32)).astype(BF16)
    return a, b, c


def _dot(a, b):
    return jnp.dot(a, b, preferred_element_type=F32)


def _inproj_kernel(x_ref, g_ref, w_ref, qk_ref, v_ref, o_ref, qs_ref, ks_ref, vs_ref,
                   gi_ref, gf_ref):
    x = x_ref[...]
    ms = jnp.mean(x * x, axis=-1, keepdims=True)
    xn = (x * lax.rsqrt(ms + EPS) * g_ref[...]).astype(BF16)

    def mm(lo, hi):
        return _dot(xn, w_ref[:, lo:hi])

    qk_ref[...] = mm(C_QK, C_V)
    v_ref[...] = mm(C_V, C_O).astype(BF16)
    o_ref[...] = mm(C_O, C_QS)
    qs_ref[...] = (mm(C_QS, C_KS) * (DH_SB ** -0.5)).astype(BF16)
    ks_ref[...] = mm(C_KS, C_VS).astype(BF16)
    vs_ref[...] = mm(C_VS, C_GI).astype(BF16)
    gi_ref[...] = mm(C_GI, C_GF)
    gf_ref[...] = mm(C_GF, C_END)


def _inproj(h, g, w, tm):
    T = h.shape[0]
    row = lambda n: pl.BlockSpec((tm, n), lambda i: (i, 0))
    const = lambda a: pl.BlockSpec(a.shape, lambda i: (0, 0))
    out_shape = (
        jax.ShapeDtypeStruct((T, 1024), F32),
        jax.ShapeDtypeStruct((T, 512), BF16),
        jax.ShapeDtypeStruct((T, 512), F32),
        jax.ShapeDtypeStruct((T, 512), BF16),
        jax.ShapeDtypeStruct((T, 512), BF16),
        jax.ShapeDtypeStruct((T, 512), BF16),
        jax.ShapeDtypeStruct((T, LANE), F32),
        jax.ShapeDtypeStruct((T, LANE), F32),
    )
    return pl.pallas_call(
        _inproj_kernel,
        grid=(T // tm,),
        in_specs=[row(1024), const(g), const(w)],
        out_specs=(row(1024), row(512), row(512), row(512), row(512), row(512),
                   row(LANE), row(LANE)),
        out_shape=out_shape,
        compiler_params=_params(("parallel",)),
        name="inproj",
    )(h, g, w)


def _mlstm_kernel(qk_ref, v_ref, o_ref, gi_ref, gf_ref, cw_ref, cb_ref, bi_ref, bf_ref,
                  gm_ref, tri_ref, out_ref, xpad, cext, mst, *, ts):
    s_idx = pl.program_id(1)

    @pl.when(s_idx == 0)
    def _():
        xpad[0:SUBLANE, :] = jnp.zeros((SUBLANE, 2 * D_MLSTM), F32)
        cext[...] = jnp.zeros(cext.shape, F32)
        mst[...] = jnp.full(mst.shape, M_INIT, F32)

    xpad[SUBLANE:SUBLANE + ts, :] = qk_ref[...]
    y = cb_ref[...]
    for tap in range(CONV_K):
        off = SUBLANE - (CONV_K - 1) + tap
        y = y + xpad[off:off + ts, :] * cw_ref[tap:tap + 1, :]
    xpad[0:SUBLANE, :] = xpad[ts:ts + SUBLANE, :]
    act = y * _sigmoid(y)
    q_all = act[:, :D_MLSTM].astype(BF16)
    kt_all = (act[:, D_MLSTM:] * (DH_MLSTM ** -0.5)).T

    row = lax.broadcasted_iota(jnp.int32, (CHUNK, CHUNK), 0)
    col = lax.broadcasted_iota(jnp.int32, (CHUNK, CHUNK), 1)
    causal = col <= row
    ones_blk = jnp.ones((CHUNK, DH_MLSTM), BF16)
    tri = tri_ref[...]

    for c in range(ts // CHUNK):
        r0 = c * CHUNK
        gi = gi_ref[r0:r0 + CHUNK, :] + bi_ref[...]
        gf = gf_ref[r0:r0 + CHUNK, :] + bf_ref[...]
        lf = jnp.minimum(gf, 0.0) - jnp.log(1.0 + jnp.exp(-jnp.abs(gf)))
        l1, l2, l3 = _split3(lf)
        bcum = _dot(tri, l1) + _dot(tri, l2) + _dot(tri, l3)
        a_all = gi - bcum
        a_t = a_all.T
        for h in range(H_MLSTM):
            hs = slice(h * DH_MLSTM, (h + 1) * DH_MLSTM)
            a_row = a_t[h:h + 1, :]
            m_prev = mst[h:h + 1, 0:1]
            mx = jnp.max(jnp.where(causal, a_row, -jnp.inf), axis=-1, keepdims=True)
            big_m = jnp.maximum(m_prev, mx)
            w = jnp.where(causal, jnp.exp(a_row - big_m), 0.0)
            a_inter = jnp.exp(m_prev - big_m)
            m_t = bcum[:, h:h + 1] + big_m
            qh = q_all[r0:r0 + CHUNK, hs]
            kt = kt_all[hs, r0:r0 + CHUNK]
            sb = (_dot(qh, kt.astype(BF16)) * w).astype(BF16)
            vext = jnp.concatenate([v_ref[r0:r0 + CHUNK, hs], ones_blk], axis=1)
            ce = cext[h]
            numext = a_inter * _dot(qh, ce.astype(BF16)) + _dot(sb, vext)
            num = numext[:, :DH_MLSTM]
            den = numext[:, DH_MLSTM:]
            hh = num / jnp.maximum(jnp.abs(den), jnp.exp(-m_t))
            ms = jnp.mean(hh * hh, axis=-1, keepdims=True)
            yh = hh * lax.rsqrt(ms + EPS) * gm_ref[:, hs]
            out_ref[r0:r0 + CHUNK, hs] = (_sigmoid(o_ref[r0:r0 + CHUNK, hs]) * yh).astype(BF16)
            m_last = big_m[CHUNK - 1:CHUNK, :]
            wkt = (kt * jnp.exp(a_row - m_last)).astype(BF16)
            cext[h] = jnp.exp(m_prev - m_last) * ce + _dot(wkt, vext)
            mst[h:h + 1, :] = jnp.broadcast_to(m_t[CHUNK - 1:CHUNK, :], (1, LANE))


def _mlstm(qk, v, o, gi, gf, conv_w, conv_b, bi, bf, gm, tri, B, S, ts):
    T = B * S
    nsb = S // ts
    row = lambda n: pl.BlockSpec((ts, n), lambda b, s: (b * nsb + s, 0))
    const = lambda a: pl.BlockSpec(a.shape, lambda b, s: (0, 0))
    return pl.pallas_call(
        functools.partial(_mlstm_kernel, ts=ts),
        grid=(B, nsb),
        in_specs=[row(1024), row(512), row(512), row(LANE), row(LANE),
                  const(conv_w), const(conv_b), const(bi), const(bf), const(gm), const(tri)],
        out_specs=row(512),
        out_shape=jax.ShapeDtypeStruct((T, D_MLSTM), BF16),
        scratch_shapes=[
            pltpu.VMEM((ts + SUBLANE, 2 * D_MLSTM), F32),
            pltpu.VMEM((H_MLSTM, DH_MLSTM, 2 * DH_MLSTM), F32),
            pltpu.VMEM((SUBLANE, LANE), F32),
        ],
        compiler_params=_params(("parallel", "arbitrary")),
        name="mlstm",
    )(qk, v, o, gi, gf, conv_w, conv_b, bi, bf, gm, tri)


def _sb_kernel(q_ref, k_ref, v_ref, g_ref, u_ref, out_ref, r_scr, acc_scr, *, tq, npb):
    qi = pl.program_id(2)
    lane = lax.broadcasted_iota(jnp.int32, (tq, LANE), 1)
    first = lane < DH_SB
    r_scr[...] = jnp.zeros(r_scr.shape, F32)
    acc_scr[...] = jnp.zeros(acc_scr.shape, F32)

    def head_q(p, e):
        q = q_ref[:, p * LANE:(p + 1) * LANE]
        zero = jnp.zeros_like(q)
        return jnp.where(first, q, zero) if e == 0 else jnp.where(first, zero, q)

    def sweep(g, diag):
        st = pl.multiple_of(g * tq, tq)
        heads = [(p, e) for p in range(npb) for e in range(2)]
        if diag:
            strict = (lax.broadcasted_iota(jnp.int32, (tq, tq), 1)
                      < lax.broadcasted_iota(jnp.int32, (tq, tq), 0))
        n = len(heads)
        zs, lbs, xs = [None] * n, [None] * n, [None] * n

        def scores(i):
            p, e = heads[i]
            zs[i] = lax.dot_general(head_q(p, e), k_ref[pl.ds(st, tq), p * LANE:(p + 1) * LANE],
                                    (((1,), (1,)), ((), ())), preferred_element_type=F32)

        def logs(i):
            z = zs[i]
            sp = jnp.log(1.0 + jnp.exp(-jnp.abs(z)))
            lb = jnp.minimum(z, 0.0) - sp
            lr = lb - z
            if diag:
                lr = jnp.where(strict, lr, 0.0)
            lbs[i] = lb
            xs[i] = _dot(lr.astype(BF16), u_ref[...])

        def weights(i):
            p, e = heads[i]
            r = r_scr[i]
            a = jnp.exp(lbs[i] + xs[i][:, :tq] + jnp.concatenate([r] * (tq // LANE), axis=1))
            if diag:
                a = jnp.where(strict, a, 0.0)
            r_scr[i] = r + xs[i][:, tq:]
            acc_scr[i] += _dot(a.astype(BF16), v_ref[pl.ds(st, tq), p * LANE:(p + 1) * LANE])

        for s in range(n + 2 * SKEW):
            if s < n:
                scores(s)
            if 0 <= s - SKEW < n:
                logs(s - SKEW)
            if 0 <= s - 2 * SKEW < n:
                weights(s - 2 * SKEW)

    sweep(qi, True)

    def body(j, c):
        sweep(qi - 1 - j, False)
        return c

    lax.fori_loop(0, qi, body, 0)

    for p in range(npb):
        o = jnp.where(first, acc_scr[2 * p], acc_scr[2 * p + 1])
        sq = o * o
        s0 = jnp.sum(jnp.where(first, sq, 0.0), axis=-1, keepdims=True)
        s1 = jnp.sum(jnp.where(first, 0.0, sq), axis=-1, keepdims=True)
        ms = jnp.where(first, s0, s1) * (1.0 / DH_SB)
        ls = slice(p * LANE, (p + 1) * LANE)
        out_ref[:, ls] = (o * lax.rsqrt(ms + EPS) * g_ref[:, ls]).astype(BF16)


def _sb_attention(qs, ks, vs, g_sb, usuf, B, S, tq, npb):
    T = B * S
    nq = S // tq
    w = npb * LANE
    return pl.pallas_call(
        functools.partial(_sb_kernel, tq=tq, npb=npb),
        grid=(B, D_SB // w, nq),
        in_specs=[
            pl.BlockSpec((tq, w), lambda b, p, i: (b * nq + i, p)),
            pl.BlockSpec((S, w), lambda b, p, i: (b, p)),
            pl.BlockSpec((S, w), lambda b, p, i: (b, p)),
            pl.BlockSpec((1, w), lambda b, p, i: (0, p)),
            pl.BlockSpec(usuf.shape, lambda b, p, i: (0, 0)),
        ],
        out_specs=pl.BlockSpec((tq, w), lambda b, p, i: (b * nq + i, p)),
        out_shape=jax.ShapeDtypeStruct((T, D_SB), BF16),
        scratch_shapes=[pltpu.VMEM((2 * npb, tq, LANE), F32),
                        pltpu.VMEM((2 * npb, tq, LANE), F32)],
        compiler_params=_params(("parallel", "parallel", "arbitrary")),
        name="sb_attention",
    )(qs, ks, vs, g_sb, usuf)


def _outproj_kernel(hm_ref, hs_ref, w_ref, h_ref, g_ref, *rest, moe):
    if moe:
        wrh_ref, wrl_ref, hn_ref, yb_ref, yf_ref, ti_ref, tw_ref = rest
    else:
        hn_ref, yb_ref = rest
    hn = h_ref[...] + _dot(hm_ref[...], w_ref[0:D_MLSTM, :]) + _dot(hs_ref[...], w_ref[D_MLSTM:, :])
    hn_ref[...] = hn
    ms = jnp.mean(hn * hn, axis=-1, keepdims=True)
    y = hn * lax.rsqrt(ms + EPS) * g_ref[...]
    yb = y.astype(BF16)
    yb_ref[...] = yb
    if moe:
        _to_row_tiles(yf_ref, yb.astype(F32))
        yl = (y - yb.astype(F32)).astype(BF16)
        logits = _dot(yb, wrh_ref[...]) + _dot(yl, wrh_ref[...]) + _dot(yb, wrl_ref[...])
        lane = lax.broadcasted_iota(jnp.int32, logits.shape, 1)
        lanef = lane.astype(F32)
        lg = jnp.where(lane < N_EXPERTS, logits, -jnp.inf)
        m1 = jnp.max(lg, axis=-1, keepdims=True)
        i1 = jnp.min(jnp.where(lg == m1, lanef, float(LANE)), axis=-1, keepdims=True)
        lg2 = jnp.where(lanef == i1, -jnp.inf, lg)
        m2 = jnp.max(lg2, axis=-1, keepdims=True)
        i2 = jnp.min(jnp.where(lg2 == m2, lanef, float(LANE)), axis=-1, keepdims=True)
        t = jnp.exp(m2 - m1)
        w1 = 1.0 / (1.0 + t)
        w2 = t * w1
        ti_ref[...] = jnp.where(lane == 0, i1, jnp.where(lane == 1, i2, 0.0)).astype(jnp.int32)
        tw_ref[...] = jnp.where(lane == 0, w1, jnp.where(lane == 1, w2, 0.0))


def _outproj(hm, hs, w_out, h, g, tm, router=None):
    T = h.shape[0]
    moe = router is not None
    row = lambda n: pl.BlockSpec((tm, n), lambda i: (i, 0))
    const = lambda a: pl.BlockSpec(a.shape, lambda i: (0, 0))
    ins = [hm, hs, w_out, h, g]
    in_specs = [row(512), row(512), const(w_out), row(1024), const(g)]
    out_shape = [jax.ShapeDtypeStruct((T, D_MODEL), F32), jax.ShapeDtypeStruct((T, D_MODEL), BF16)]
    out_specs = [row(1024), row(1024)]
    if moe:
        ins += list(router)
        in_specs += [const(router[0]), const(router[1])]
        out_shape += [jax.ShapeDtypeStruct((T * SUBLANE, LANE), F32),
                      jax.ShapeDtypeStruct((T, LANE), jnp.int32),
                      jax.ShapeDtypeStruct((T, LANE), F32)]
        out_specs += [pl.BlockSpec((tm * SUBLANE, LANE), lambda i: (i, 0)), row(LANE), row(LANE)]
    return pl.pallas_call(
        functools.partial(_outproj_kernel, moe=moe),
        grid=(T // tm,),
        in_specs=in_specs,
        out_specs=tuple(out_specs),
        out_shape=tuple(out_shape),
        compiler_params=_params(("parallel",)),
        name="outproj_moe" if moe else "outproj",
    )(*ins)


def _ffn_kernel(y_ref, h_ref, wg_ref, wu_ref, wd_ref, out_ref, *, fc):
    y = y_ref[...]
    out_ref[...] = h_ref[...]
    for c in range(wg_ref.shape[1] // fc):
        cs = slice(c * fc, (c + 1) * fc)
        g = _dot(y, wg_ref[:, cs])
        u = _dot(y, wu_ref[:, cs])
        a = (g * _sigmoid(g) * u).astype(BF16)
        out_ref[...] += _dot(a, wd_ref[cs, :])


def _ffn(y, h, wg, wu, wd, tm, fc):
    T = h.shape[0]
    row = lambda n: pl.BlockSpec((tm, n), lambda i: (i, 0))
    const = lambda a: pl.BlockSpec(a.shape, lambda i: (0, 0))
    return pl.pallas_call(
        functools.partial(_ffn_kernel, fc=fc),
        grid=(T // tm,),
        in_specs=[row(1024), row(1024), const(wg), const(wu), const(wd)],
        out_specs=row(1024),
        out_shape=jax.ShapeDtypeStruct((T, D_MODEL), F32),
        compiler_params=_params(("parallel",)),
        name="ffn",
    )(y, h, wg, wu, wd)


def _row_copy(src, dst, sem, s, d):
    return pltpu.make_async_copy(
        src.at[pl.ds(pl.multiple_of(s * SUBLANE, SUBLANE), SUBLANE), :],
        dst.at[pl.ds(pl.multiple_of(d * SUBLANE, SUBLANE), SUBLANE), :], sem)


def _to_row_tiles(ref, x):
    n = x.shape[0]
    for s in range(D_MODEL // LANE):
        ref[pl.ds(s, n, stride=SUBLANE), :] = x[:, s * LANE:(s + 1) * LANE]


def _from_row_tiles(ref, n, s):
    return ref[pl.ds(s, n, stride=SUBLANE), :]


def _expert_kernel(te_ref, tr_ref, idx_ref, nidx_ref, y_hbm, wg_ref, wu_ref, wd_ref, out_ref,
                   xbuf, xb, acc, sem, *, tm, chunk):
    i = pl.program_id(0)
    f = pl.program_id(1)
    slot = lax.rem(i, 2)

    def gather(idx, s, lo, groups):
        def issue(k, c):
            base = pl.multiple_of(lo + k * SUBLANE, SUBLANE)
            for u in range(SUBLANE):
                _row_copy(y_hbm, xbuf.at[s], sem.at[s], idx[0, 0, base + u], base + u).start()
            return c
        lax.fori_loop(0, groups, issue, 0)

    @pl.when(f == 0)
    def _():
        @pl.when(i == 0)
        def _():
            gather(idx_ref, 0, 0, tm // SUBLANE)

        def drain(r, c):
            _row_copy(y_hbm, xbuf.at[slot], sem.at[slot], 0, r).wait()
            return c
        lax.fori_loop(0, tm, drain, 0, unroll=8)
        for s in range(D_MODEL // LANE):
            xb[:, s * LANE:(s + 1) * LANE] = _from_row_tiles(xbuf.at[slot], tm, s).astype(BF16)
        acc[...] = jnp.zeros(acc.shape, F32)

    @pl.when(i + 1 < pl.num_programs(0))
    def _():
        lo = f * chunk
        gather(nidx_ref, 1 - slot, lo, (jnp.minimum(lo + chunk, tm) - lo) // SUBLANE)

    def swiglu(nrows):
        x = xb[0:nrows, :]
        g = _dot(x, wg_ref[...].astype(BF16))
        u = _dot(x, wu_ref[...].astype(BF16))
        a = (g * _sigmoid(g) * u).astype(BF16)
        acc[0:nrows, :] += _dot(a, wd_ref[...].astype(BF16))

    valid = tr_ref[i]

    @pl.when(valid > tm // 2)
    def _():
        swiglu(tm)

    @pl.when((valid > 0) & (valid <= tm // 2))
    def _():
        swiglu(tm // 2)

    @pl.when(f == pl.num_programs(1) - 1)
    def _():
        _to_row_tiles(out_ref, acc[...])


def _experts(y, tok_of_slot, tile_expert, tile_rows, wg, wu, wd, layer, tm, tf):
    P = tok_of_slot.shape[0]
    n = P // tm
    F = wg.shape[3]
    idx = tok_of_slot.reshape(n, 1, tm)
    rows_per_step = -(-tm // (F // tf))
    chunk = -(-rows_per_step // SUBLANE) * SUBLANE
    grid_spec = pltpu.PrefetchScalarGridSpec(
        num_scalar_prefetch=2,
        grid=(n, F // tf),
        in_specs=[
            pl.BlockSpec((1, 1, tm), lambda i, f, te, tr: (i, 0, 0), memory_space=pltpu.SMEM),
            pl.BlockSpec((1, 1, tm), lambda i, f, te, tr: (jnp.minimum(i + 1, n - 1), 0, 0),
                         memory_space=pltpu.SMEM),
            pl.BlockSpec(memory_space=pl.ANY),
            pl.BlockSpec((None, None, D_MODEL, tf), lambda i, f, te, tr: (layer, te[i], 0, f)),
            pl.BlockSpec((None, None, D_MODEL, tf), lambda i, f, te, tr: (layer, te[i], 0, f)),
            pl.BlockSpec((None, None, tf, D_MODEL), lambda i, f, te, tr: (layer, te[i], f, 0)),
        ],
        out_specs=pl.BlockSpec((tm * SUBLANE, LANE), lambda i, f, te, tr: (i, 0)),
        scratch_shapes=[pltpu.VMEM((2, tm * SUBLANE, LANE), F32), pltpu.VMEM((tm, D_MODEL), BF16),
                        pltpu.VMEM((tm, D_MODEL), F32), pltpu.SemaphoreType.DMA((2,))],
    )
    return pl.pallas_call(
        functools.partial(_expert_kernel, tm=tm, chunk=chunk),
        grid_spec=grid_spec,
        out_shape=jax.ShapeDtypeStruct((P * SUBLANE, LANE), F32),
        compiler_params=_params(("arbitrary", "arbitrary")),
        name="moe_experts",
    )(tile_expert, tile_rows, idx, idx, y, wg, wu, wd)


def _combine_kernel(s0_ref, s1_ref, e_hbm, h_ref, tw_ref, out_ref, b0, b1, sem, *, rows):
    def issue(r, c):
        _row_copy(e_hbm, b0, sem.at[0], s0_ref[0, 0, r], r).start()
        _row_copy(e_hbm, b1, sem.at[1], s1_ref[0, 0, r], r).start()
        return c

    lax.fori_loop(0, rows, issue, 0, unroll=8)

    def drain(r, c):
        _row_copy(e_hbm, b0, sem.at[0], 0, r).wait()
        _row_copy(e_hbm, b1, sem.at[1], 0, r).wait()
        return c

    lax.fori_loop(0, rows, drain, 0, unroll=8)
    tw = tw_ref[...]
    for s in range(D_MODEL // LANE):
        cs = slice(s * LANE, (s + 1) * LANE)
        out_ref[:, cs] = (h_ref[:, cs] + tw[:, 0:1] * _from_row_tiles(b0, rows, s)
                          + tw[:, 1:2] * _from_row_tiles(b1, rows, s))


def _combine(e_sorted, slot0, slot1, h, topw, rows):
    T = h.shape[0]
    n = T // rows
    smem = lambda: pl.BlockSpec((1, 1, rows), lambda i: (i, 0, 0), memory_space=pltpu.SMEM)
    return pl.pallas_call(
        functools.partial(_combine_kernel, rows=rows),
        grid=(n,),
        in_specs=[smem(), smem(), pl.BlockSpec(memory_space=pl.ANY),
                  pl.BlockSpec((rows, D_MODEL), lambda i: (i, 0)),
                  pl.BlockSpec((rows, LANE), lambda i: (i, 0))],
        out_specs=pl.BlockSpec((rows, D_MODEL), lambda i: (i, 0)),
        out_shape=jax.ShapeDtypeStruct((T, D_MODEL), F32),
        scratch_shapes=[pltpu.VMEM((rows * SUBLANE, LANE), F32),
                        pltpu.VMEM((rows * SUBLANE, LANE), F32),
                        pltpu.SemaphoreType.DMA((2,))],
        compiler_params=_params(("arbitrary",)),
        name="moe_combine",
    )(slot0.reshape(n, 1, rows), slot1.reshape(n, 1, rows), e_sorted, h, topw)


def _route(topi, tm):
    T = topi.shape[0]
    e = topi[:, :2].reshape(-1)
    onehot = (e[:, None] == jnp.arange(N_EXPERTS, dtype=jnp.int32)[None, :]).astype(jnp.int32)
    csum = jnp.cumsum(onehot, axis=0)
    pos = jnp.sum((csum - onehot) * onehot, axis=1)
    counts = csum[-1]
    tiles = (counts + tm - 1) // tm
    tile_end = jnp.cumsum(tiles)
    offs = (tile_end - tiles) * tm
    slot = jnp.sum(onehot * offs[None, :], axis=1) + pos
    n_tiles = 2 * T // tm + N_EXPERTS
    tile_ids = jnp.arange(n_tiles, dtype=jnp.int32)
    tile_expert = jnp.sum((tile_ids[:, None] >= tile_end[None, :]).astype(jnp.int32), axis=1)
    tile_expert = jnp.minimum(tile_expert, N_EXPERTS - 1).astype(jnp.int32)
    first_tile = (tile_end - tiles)[tile_expert]
    tile_rows = jnp.clip(counts[tile_expert] - (tile_ids - first_tile) * tm, 0, tm)
    tile_rows = jnp.where(tile_ids < tile_end[-1], tile_rows, 0).astype(jnp.int32)
    tok_of_slot = jnp.zeros((n_tiles * tm,), jnp.int32).at[slot].set(
        jnp.arange(2 * T, dtype=jnp.int32) // 2)
    slot2 = slot.reshape(T, 2)
    return tok_of_slot, tile_expert, tile_rows, slot2[:, 0], slot2[:, 1]


def _norm_kernel(x_ref, g_ref, o_ref):
    x = x_ref[...]
    ms = jnp.mean(x * x, axis=-1, keepdims=True)
    o_ref[...] = x * lax.rsqrt(ms + EPS) * g_ref[...]


def _final_norm(h, g, tm):
    T = h.shape[0]
    return pl.pallas_call(
        _norm_kernel,
        grid=(T // tm,),
        in_specs=[pl.BlockSpec((tm, D_MODEL), lambda i: (i, 0)),
                  pl.BlockSpec((1, D_MODEL), lambda i: (0, 0))],
        out_specs=pl.BlockSpec((tm, D_MODEL), lambda i: (i, 0)),
        out_shape=jax.ShapeDtypeStruct((T, D_MODEL), F32),
        compiler_params=_params(("parallel",)),
        name="final_norm",
    )(h, g)


def _pad_cols(w, n):
    return jnp.pad(w, ((0, 0), (0, n - w.shape[1])))


def _prep_w_in(w):
    gates0 = 4 * D_MLSTM
    gi = _pad_cols(w[:, gates0:gates0 + H_MLSTM], LANE)
    gf = _pad_cols(w[:, gates0 + H_MLSTM:gates0 + 2 * H_MLSTM], LANE)
    rest = w[:, gates0 + 2 * H_MLSTM:]
    return jnp.concatenate([w[:, :gates0], rest, gi, gf], axis=1).astype(BF16)


def kernel(x, norm_mix_g, w_in, b_igate, b_fgate, conv_w, conv_b, g_mlstm, g_sb, w_out,
           norm_ffn_g, ffn_w_gate, ffn_w_up, ffn_w_down, w_router, moe_w_gate, moe_w_up,
           moe_w_down, norm_final_g):
    B, S, D = x.shape
    T = B * S
    depth = w_in.shape[0]
    tm = min(512, T)
    ts = min(256, S)
    tq = min(256, S)
    tm_moe = min(1024, T)
    rows = min(1024, T)

    ii = jnp.arange(CHUNK)
    tri = (ii[None, :] <= ii[:, None]).astype(BF16)
    jj = jnp.arange(tq)
    upper = (jj[:, None] > jj[None, :]).astype(BF16)
    usuf = jnp.concatenate([upper, jnp.ones((tq, LANE), BF16)], axis=1)

    f_dense = ffn_w_gate.shape[2]
    f_pad = -(-f_dense // 256) * 256

    h = x.reshape(T, D)
    for layer in range(depth):
        w1 = _prep_w_in(w_in[layer])
        qk, v_m, o_m, q_s, k_s, v_s, gi, gf = _inproj(h, norm_mix_g[layer][None, :], w1, tm)
        bi = _pad_cols(b_igate[layer][None, :], LANE)
        bf = _pad_cols(b_fgate[layer][None, :], LANE)
        h_m = _mlstm(qk, v_m, o_m, gi, gf, conv_w[layer], conv_b[layer][None, :], bi, bf,
                     g_mlstm[layer][None, :], tri, B, S, ts)
        h_s = _sb_attention(q_s, k_s, v_s, g_sb[layer][None, :], usuf, B, S, tq, 4)
        wo = w_out[layer].astype(BF16)
        gffn = norm_ffn_g[layer][None, :]
        j = layer // 2
        if layer % 2 == 0:
            h, yb = _outproj(h_m, h_s, wo, h, gffn, tm)
            wg = _pad_cols(ffn_w_gate[j], f_pad).astype(BF16)
            wu = _pad_cols(ffn_w_up[j], f_pad).astype(BF16)
            wd = jnp.pad(ffn_w_down[j], ((0, f_pad - f_dense), (0, 0))).astype(BF16)
            h = _ffn(yb, h, wg, wu, wd, tm, 256)
        else:
            wr = _pad_cols(w_router[j], LANE)
            wrh = wr.astype(BF16)
            wrl = (wr - wrh.astype(F32)).astype(BF16)
            h, yb, yf, topi, topw = _outproj(h_m, h_s, wo, h, gffn, tm, router=(wrh, wrl))
            tok_of_slot, tile_expert, tile_rows, slot0, slot1 = _route(topi, tm_moe)
            e_sorted = _experts(yf, tok_of_slot, tile_expert, tile_rows, moe_w_gate, moe_w_up,
                                moe_w_down, j, tm_moe, 512)
            h = _combine(e_sorted, slot0, slot1, h, topw, rows)
    out = _final_norm(h, norm_final_g[None, :], tm)
    return out.reshape(B, S, D)
```

```python
import functools

import jax
import jax.numpy as jnp
from jax import lax
from jax.experimental import pallas as pl
from jax.experimental.pallas import tpu as pltpu

F32 = jnp.float32
BF16 = jnp.bfloat16

D_MODEL = 1024
D_MLSTM = 512
H_MLSTM = 4
DH_MLSTM = 128
D_SB = 512
H_SB = 8
DH_SB = 64
CONV_K = 4
N_EXPERTS = 8
EPS = 1e-6
M_INIT = -1e30
SKEW = 1

LANE = 128
SUBLANE = 8
assert D_MODEL == SUBLANE * LANE
CHUNK = 128
VMEM_LIMIT = 52 * 1024 * 1024

C_QK, C_V, C_O, C_QS, C_KS, C_VS, C_GI, C_GF, C_END = (
    0, 1024, 1536, 2048, 2560, 3072, 3584, 3712, 3840)


def _params(sem, **kw):
    return pltpu.CompilerParams(dimension_semantics=sem, vmem_limit_bytes=VMEM_LIMIT, **kw)


def _sigmoid(x):
    return 1.0 / (1.0 + jnp.exp(-x))


def _split3(x):
    a = x.astype(BF16)
    r = x - a.astype(F32)
    b = r.astype(BF16)
    c = (r - b.astype(F32)).astype(BF16)
    return a, b, c


def _dot(a, b):
    return jnp.dot(a, b, preferred_element_type=F32)


def _inproj_kernel(x_ref, g_ref, w_ref, qk_ref, v_ref, o_ref, qs_ref, ks_ref, vs_ref,
                   gi_ref, gf_ref):
    x = x_ref[...]
    ms = jnp.mean(x * x, axis=-1, keepdims=True)
    xn = (x * lax.rsqrt(ms + EPS) * g_ref[...]).astype(BF16)

    def mm(lo, hi):
        return _dot(xn, w_ref[:, lo:hi])

    qk_ref[...] = mm(C_QK, C_V)
    v_ref[...] = mm(C_V, C_O).astype(BF16)
    o_ref[...] = mm(C_O, C_QS)
    qs_ref[...] = (mm(C_QS, C_KS) * (DH_SB ** -0.5)).astype(BF16)
    ks_ref[...] = mm(C_KS, C_VS).astype(BF16)
    vs_ref[...] = mm(C_VS, C_GI).astype(BF16)
    gi_ref[...] = mm(C_GI, C_GF)
    gf_ref[...] = mm(C_GF, C_END)


def _inproj(h, g, w, tm):
    T = h.shape[0]
    row = lambda n: pl.BlockSpec((tm, n), lambda i: (i, 0))
    const = lambda a: pl.BlockSpec(a.shape, lambda i: (0, 0))
    out_shape = (
        jax.ShapeDtypeStruct((T, 1024), F32),
        jax.ShapeDtypeStruct((T, 512), BF16),
        jax.ShapeDtypeStruct((T, 512), F32),
        jax.ShapeDtypeStruct((T, 512), BF16),
        jax.ShapeDtypeStruct((T, 512), BF16),
        jax.ShapeDtypeStruct((T, 512), BF16),
        jax.ShapeDtypeStruct((T, LANE), F32),
        jax.ShapeDtypeStruct((T, LANE), F32),
    )
    return pl.pallas_call(
        _inproj_kernel,
        grid=(T // tm,),
        in_specs=[row(1024), const(g), const(w)],
        out_specs=(row(1024), row(512), row(512), row(512), row(512), row(512),
                   row(LANE), row(LANE)),
        out_shape=out_shape,
        compiler_params=_params(("parallel",)),
        name="inproj",
    )(h, g, w)


def _mlstm_kernel(qk_ref, v_ref, o_ref, gi_ref, gf_ref, cw_ref, cb_ref, bi_ref, bf_ref,
                  gm_ref, tri_ref, out_ref, xpad, cext, mst, *, ts):
    s_idx = pl.program_id(1)

    @pl.when(s_idx == 0)
    def _():
        xpad[0:SUBLANE, :] = jnp.zeros((SUBLANE, 2 * D_MLSTM), F32)
        cext[...] = jnp.zeros(cext.shape, F32)
        mst[...] = jnp.full(mst.shape, M_INIT, F32)

    xpad[SUBLANE:SUBLANE + ts, :] = qk_ref[...]
    y = cb_ref[...]
    for tap in range(CONV_K):
        off = SUBLANE - (CONV_K - 1) + tap
        y = y + xpad[off:off + ts, :] * cw_ref[tap:tap + 1, :]
    xpad[0:SUBLANE, :] = xpad[ts:ts + SUBLANE, :]
    act = y * _sigmoid(y)
    q_all = act[:, :D_MLSTM].astype(BF16)
    kt_all = (act[:, D_MLSTM:] * (DH_MLSTM ** -0.5)).T

    row = lax.broadcasted_iota(jnp.int32, (CHUNK, CHUNK), 0)
    col = lax.broadcasted_iota(jnp.int32, (CHUNK, CHUNK), 1)
    causal = col <= row
    ones_blk = jnp.ones((CHUNK, DH_MLSTM), BF16)
    tri = tri_ref[...]

    for c in range(ts // CHUNK):
        r0 = c * CHUNK
        gi = gi_ref[r0:r0 + CHUNK, :] + bi_ref[...]
        gf = gf_ref[r0:r0 + CHUNK, :] + bf_ref[...]
        lf = jnp.minimum(gf, 0.0) - jnp.log(1.0 + jnp.exp(-jnp.abs(gf)))
        l1, l2, l3 = _split3(lf)
        bcum = _dot(tri, l1) + _dot(tri, l2) + _dot(tri, l3)
        a_all = gi - bcum
        a_t = a_all.T
        for h in range(H_MLSTM):
            hs = slice(h * DH_MLSTM, (h + 1) * DH_MLSTM)
            a_row = a_t[h:h + 1, :]
            m_prev = mst[h:h + 1, 0:1]
            mx = jnp.max(jnp.where(causal, a_row, -jnp.inf), axis=-1, keepdims=True)
            big_m = jnp.maximum(m_prev, mx)
            w = jnp.where(causal, jnp.exp(a_row - big_m), 0.0)
            a_inter = jnp.exp(m_prev - big_m)
            m_t = bcum[:, h:h + 1] + big_m
            qh = q_all[r0:r0 + CHUNK, hs]
            kt = kt_all[hs, r0:r0 + CHUNK]
            sb = (_dot(qh, kt.astype(BF16)) * w).astype(BF16)
            vext = jnp.concatenate([v_ref[r0:r0 + CHUNK, hs], ones_blk], axis=1)
            ce = cext[h]
            numext = a_inter * _dot(qh, ce.astype(BF16)) + _dot(sb, vext)
            num = numext[:, :DH_MLSTM]
            den = numext[:, DH_MLSTM:]
            hh = num / jnp.maximum(jnp.abs(den), jnp.exp(-m_t))
            ms = jnp.mean(hh * hh, axis=-1, keepdims=True)
            yh = hh * lax.rsqrt(ms + EPS) * gm_ref[:, hs]
            out_ref[r0:r0 + CHUNK, hs] = (_sigmoid(o_ref[r0:r0 + CHUNK, hs]) * yh).astype(BF16)
            m_last = big_m[CHUNK - 1:CHUNK, :]
            wkt = (kt * jnp.exp(a_row - m_last)).astype(BF16)
            cext[h] = jnp.exp(m_prev - m_last) * ce + _dot(wkt, vext)
            mst[h:h + 1, :] = jnp.broadcast_to(m_t[CHUNK - 1:CHUNK, :], (1, LANE))


def _mlstm(qk, v, o, gi, gf, conv_w, conv_b, bi, bf, gm, tri, B, S, ts):
    T = B * S
    nsb = S // ts
    row = lambda n: pl.BlockSpec((ts, n), lambda b, s: (b * nsb + s, 0))
    const = lambda a: pl.BlockSpec(a.shape, lambda b, s: (0, 0))
    return pl.pallas_call(
        functools.partial(_mlstm_kernel, ts=ts),
        grid=(B, nsb),
        in_specs=[row(1024), row(512), row(512), row(LANE), row(LANE),
                  const(conv_w), const(conv_b), const(bi), const(bf), const(gm), const(tri)],
        out_specs=row(512),
        out_shape=jax.ShapeDtypeStruct((T, D_MLSTM), BF16),
        scratch_shapes=[
            pltpu.VMEM((ts + SUBLANE, 2 * D_MLSTM), F32),
            pltpu.VMEM((H_MLSTM, DH_MLSTM, 2 * DH_MLSTM), F32),
            pltpu.VMEM((SUBLANE, LANE), F32),
        ],
        compiler_params=_params(("parallel", "arbitrary")),
        name="mlstm",
    )(qk, v, o, gi, gf, conv_w, conv_b, bi, bf, gm, tri)


def _sb_kernel(q_ref, k_ref, v_ref, g_ref, u_ref, out_ref, r_scr, acc_scr, *, tq, npb):
    qi = pl.program_id(2)
    lane = lax.broadcasted_iota(jnp.int32, (tq, LANE), 1)
    first = lane < DH_SB
    r_scr[...] = jnp.zeros(r_scr.shape, F32)
    acc_scr[...] = jnp.zeros(acc_scr.shape, F32)

    def head_q(p, e):
        q = q_ref[:, p * LANE:(p + 1) * LANE]
        zero = jnp.zeros_like(q)
        return jnp.where(first, q, zero) if e == 0 else jnp.where(first, zero, q)

    def sweep(g, diag):
        st = pl.multiple_of(g * tq, tq)
        heads = [(p, e) for p in range(npb) for e in range(2)]
        if diag:
            strict = (lax.broadcasted_iota(jnp.int32, (tq, tq), 1)
                      < lax.broadcasted_iota(jnp.int32, (tq, tq), 0))
        n = len(heads)
        zs, lbs, xs = [None] * n, [None] * n, [None] * n

        def scores(i):
            p, e = heads[i]
            zs[i] = lax.dot_general(head_q(p, e), k_ref[pl.ds(st, tq), p * LANE:(p + 1) * LANE],
                                    (((1,), (1,)), ((), ())), preferred_element_type=F32)

        def logs(i):
            z = zs[i]
            sp = jnp.log(1.0 + jnp.exp(-jnp.abs(z)))
            lb = jnp.minimum(z, 0.0) - sp
            lr = lb - z
            if diag:
                lr = jnp.where(strict, lr, 0.0)
            lbs[i] = lb
            xs[i] = _dot(lr.astype(BF16), u_ref[...])

        def weights(i):
            p, e = heads[i]
            r = r_scr[i]
            a = jnp.exp(lbs[i] + xs[i][:, :tq] + jnp.concatenate([r] * (tq // LANE), axis=1))
            if diag:
                a = jnp.where(strict, a, 0.0)
            r_scr[i] = r + xs[i][:, tq:]
            acc_scr[i] += _dot(a.astype(BF16), v_ref[pl.ds(st, tq), p * LANE:(p + 1) * LANE])

        for s in range(n + 2 * SKEW):
            if s < n:
                scores(s)
            if 0 <= s - SKEW < n:
                logs(s - SKEW)
            if 0 <= s - 2 * SKEW < n:
                weights(s - 2 * SKEW)

    sweep(qi, True)

    def body(j, c):
        sweep(qi - 1 - j, False)
        return c

    lax.fori_loop(0, qi, body, 0)

    for p in range(npb):
        o = jnp.where(first, acc_scr[2 * p], acc_scr[2 * p + 1])
        sq = o * o
        s0 = jnp.sum(jnp.where(first, sq, 0.0), axis=-1, keepdims=True)
        s1 = jnp.sum(jnp.where(first, 0.0, sq), axis=-1, keepdims=True)
        ms = jnp.where(first, s0, s1) * (1.0 / DH_SB)
        ls = slice(p * LANE, (p + 1) * LANE)
        out_ref[:, ls] = (o * lax.rsqrt(ms + EPS) * g_ref[:, ls]).astype(BF16)


def _sb_attention(qs, ks, vs, g_sb, usuf, B, S, tq, npb):
    T = B * S
    nq = S // tq
    w = npb * LANE
    return pl.pallas_call(
        functools.partial(_sb_kernel, tq=tq, npb=npb),
        grid=(B, D_SB // w, nq),
        in_specs=[
            pl.BlockSpec((tq, w), lambda b, p, i: (b * nq + i, p)),
            pl.BlockSpec((S, w), lambda b, p, i: (b, p)),
            pl.BlockSpec((S, w), lambda b, p, i: (b, p)),
            pl.BlockSpec((1, w), lambda b, p, i: (0, p)),
            pl.BlockSpec(usuf.shape, lambda b, p, i: (0, 0)),
        ],
        out_specs=pl.BlockSpec((tq, w), lambda b, p, i: (b * nq + i, p)),
        out_shape=jax.ShapeDtypeStruct((T, D_SB), BF16),
        scratch_shapes=[pltpu.VMEM((2 * npb, tq, LANE), F32),
                        pltpu.VMEM((2 * npb, tq, LANE), F32)],
        compiler_params=_params(("parallel", "parallel", "arbitrary")),
        name="sb_attention",
    )(qs, ks, vs, g_sb, usuf)


def _outproj_kernel(hm_ref, hs_ref, w_ref, h_ref, g_ref, *rest, moe):
    if moe:
        wrh_ref, wrl_ref, hn_ref, yb_ref, yf_ref, ti_ref, tw_ref = rest
    else:
        hn_ref, yb_ref = rest
    hn = h_ref[...] + _dot(hm_ref[...], w_ref[0:D_MLSTM, :]) + _dot(hs_ref[...], w_ref[D_MLSTM:, :])
    hn_ref[...] = hn
    ms = jnp.mean(hn * hn, axis=-1, keepdims=True)
    y = hn * lax.rsqrt(ms + EPS) * g_ref[...]
    yb = y.astype(BF16)
    yb_ref[...] = yb
    if moe:
        _to_row_tiles(yf_ref, yb.astype(F32))
        yl = (y - yb.astype(F32)).astype(BF16)
        logits = _dot(yb, wrh_ref[...]) + _dot(yl, wrh_ref[...]) + _dot(yb, wrl_ref[...])
        lane = lax.broadcasted_iota(jnp.int32, logits.shape, 1)
        lanef = lane.astype(F32)
        lg = jnp.where(lane < N_EXPERTS, logits, -jnp.inf)
        m1 = jnp.max(lg, axis=-1, keepdims=True)
        i1 = jnp.min(jnp.where(lg == m1, lanef, float(LANE)), axis=-1, keepdims=True)
        lg2 = jnp.where(lanef == i1, -jnp.inf, lg)
        m2 = jnp.max(lg2, axis=-1, keepdims=True)
        i2 = jnp.min(jnp.where(lg2 == m2, lanef, float(LANE)), axis=-1, keepdims=True)
        t = jnp.exp(m2 - m1)
        w1 = 1.0 / (1.0 + t)
        w2 = t * w1
        ti_ref[...] = jnp.where(lane == 0, i1, jnp.where(lane == 1, i2, 0.0)).astype(jnp.int32)
        tw_ref[...] = jnp.where(lane == 0, w1, jnp.where(lane == 1, w2, 0.0))


def _outproj(hm, hs, w_out, h, g, tm, router=None):
    T = h.shape[0]
    moe = router is not None
    row = lambda n: pl.BlockSpec((tm, n), lambda i: (i, 0))
    const = lambda a: pl.BlockSpec(a.shape, lambda i: (0, 0))
    ins = [hm, hs, w_out, h, g]
    in_specs = [row(512), row(512), const(w_out), row(1024), const(g)]
    out_shape = [jax.ShapeDtypeStruct((T, D_MODEL), F32), jax.ShapeDtypeStruct((T, D_MODEL), BF16)]
    out_specs = [row(1024), row(1024)]
    if moe:
        ins += list(router)
        in_specs += [const(router[0]), const(router[1])]
        out_shape += [jax.ShapeDtypeStruct((T * SUBLANE, LANE), F32),
                      jax.ShapeDtypeStruct((T, LANE), jnp.int32),
                      jax.ShapeDtypeStruct((T, LANE), F32)]
        out_specs += [pl.BlockSpec((tm * SUBLANE, LANE), lambda i: (i, 0)), row(LANE), row(LANE)]
    return pl.pallas_call(
        functools.partial(_outproj_kernel, moe=moe),
        grid=(T // tm,),
        in_specs=in_specs,
        out_specs=tuple(out_specs),
        out_shape=tuple(out_shape),
        compiler_params=_params(("parallel",)),
        name="outproj_moe" if moe else "outproj",
    )(*ins)


def _ffn_kernel(y_ref, h_ref, wg_ref, wu_ref, wd_ref, out_ref, *, fc):
    y = y_ref[...]
    out_ref[...] = h_ref[...]
    for c in range(wg_ref.shape[1] // fc):
        cs = slice(c * fc, (c + 1) * fc)
        g = _dot(y, wg_ref[:, cs])
        u = _dot(y, wu_ref[:, cs])
        a = (g * _sigmoid(g) * u).astype(BF16)
        out_ref[...] += _dot(a, wd_ref[cs, :])


def _ffn(y, h, wg, wu, wd, tm, fc):
    T = h.shape[0]
    row = lambda n: pl.BlockSpec((tm, n), lambda i: (i, 0))
    const = lambda a: pl.BlockSpec(a.shape, lambda i: (0, 0))
    return pl.pallas_call(
        functools.partial(_ffn_kernel, fc=fc),
        grid=(T // tm,),
        in_specs=[row(1024), row(1024), const(wg), const(wu), const(wd)],
        out_specs=row(1024),
        out_shape=jax.ShapeDtypeStruct((T, D_MODEL), F32),
        compiler_params=_params(("parallel",)),
        name="ffn",
    )(y, h, wg, wu, wd)


def _row_copy(src, dst, sem, s, d):
    return pltpu.make_async_copy(
        src.at[pl.ds(pl.multiple_of(s * SUBLANE, SUBLANE), SUBLANE), :],
        dst.at[pl.ds(pl.multiple_of(d * SUBLANE, SUBLANE), SUBLANE), :], sem)


def _to_row_tiles(ref, x):
    n = x.shape[0]
    for s in range(D_MODEL // LANE):
        ref[pl.ds(s, n, stride=SUBLANE), :] = x[:, s * LANE:(s + 1) * LANE]


def _from_row_tiles(ref, n, s):
    return ref[pl.ds(s, n, stride=SUBLANE), :]


def _dispatch_kernel(s0_ref, s1_ref, y_ref, zero_hbm, x_hbm, sem, *, rows):
    del zero_hbm

    def issue(k, c):
        base = pl.multiple_of(k * SUBLANE, SUBLANE)
        for u in range(SUBLANE):
            _row_copy(y_ref, x_hbm, sem.at[0], base + u, s0_ref[0, 0, base + u]).start()
            _row_copy(y_ref, x_hbm, sem.at[1], base + u, s1_ref[0, 0, base + u]).start()
        return c

    lax.fori_loop(0, rows // SUBLANE, issue, 0)

    def drain(r, c):
        _row_copy(y_ref, x_hbm, sem.at[0], 0, 0).wait()
        _row_copy(y_ref, x_hbm, sem.at[1], 0, 0).wait()
        return c

    lax.fori_loop(0, rows, drain, 0, unroll=8)


def _dispatch(y_tiles, slot0, slot1, n_slots, rows):
    T = slot0.shape[0]
    n = T // rows
    smem = lambda: pl.BlockSpec((1, 1, rows), lambda i: (i, 0, 0), memory_space=pltpu.SMEM)
    zeros = jnp.zeros((n_slots * SUBLANE, LANE), F32)
    return pl.pallas_call(
        functools.partial(_dispatch_kernel, rows=rows),
        grid=(n,),
        in_specs=[smem(), smem(), pl.BlockSpec((rows * SUBLANE, LANE), lambda i: (i, 0)),
                  pl.BlockSpec(memory_space=pl.ANY)],
        out_specs=pl.BlockSpec(memory_space=pl.ANY),
        out_shape=jax.ShapeDtypeStruct((n_slots * SUBLANE, LANE), F32),
        input_output_aliases={3: 0},
        scratch_shapes=[pltpu.SemaphoreType.DMA((2,))],
        compiler_params=_params(("arbitrary",)),
        name="moe_dispatch",
    )(slot0.reshape(n, 1, rows), slot1.reshape(n, 1, rows), y_tiles, zeros)


def _expert_kernel(te_ref, tr_ref, x_ref, wg_ref, wu_ref, wd_ref, out_ref, xb, acc, *, tm):
    i = pl.program_id(0)
    f = pl.program_id(1)

    @pl.when(f == 0)
    def _():
        for s in range(D_MODEL // LANE):
            xb[:, s * LANE:(s + 1) * LANE] = _from_row_tiles(x_ref, tm, s).astype(BF16)
        acc[...] = jnp.zeros(acc.shape, F32)

    def swiglu(nrows):
        x = xb[0:nrows, :]
        g = _dot(x, wg_ref[...].astype(BF16))
        u = _dot(x, wu_ref[...].astype(BF16))
        a = (g * _sigmoid(g) * u).astype(BF16)
        acc[0:nrows, :] += _dot(a, wd_ref[...].astype(BF16))

    valid = tr_ref[i]

    @pl.when(valid > tm // 2)
    def _():
        swiglu(tm)

    @pl.when((valid > 0) & (valid <= tm // 2))
    def _():
        swiglu(tm // 2)

    @pl.when(f == pl.num_programs(1) - 1)
    def _():
        _to_row_tiles(out_ref, acc[...])


def _experts(x_sorted, tile_expert, tile_rows, wg, wu, wd, layer, tm, tf):
    P = x_sorted.shape[0] // SUBLANE
    n = P // tm
    F = wg.shape[3]
    grid_spec = pltpu.PrefetchScalarGridSpec(
        num_scalar_prefetch=2,
        grid=(n, F // tf),
        in_specs=[
            pl.BlockSpec((tm * SUBLANE, LANE), lambda i, f, te, tr: (i, 0)),
            pl.BlockSpec((None, None, D_MODEL, tf), lambda i, f, te, tr: (layer, te[i], 0, f)),
            pl.BlockSpec((None, None, D_MODEL, tf), lambda i, f, te, tr: (layer, te[i], 0, f)),
            pl.BlockSpec((None, None, tf, D_MODEL), lambda i, f, te, tr: (layer, te[i], f, 0)),
        ],
        out_specs=pl.BlockSpec((tm * SUBLANE, LANE), lambda i, f, te, tr: (i, 0)),
        scratch_shapes=[pltpu.VMEM((tm, D_MODEL), BF16), pltpu.VMEM((tm, D_MODEL), F32)],
    )
    return pl.pallas_call(
        functools.partial(_expert_kernel, tm=tm),
        grid_spec=grid_spec,
        out_shape=jax.ShapeDtypeStruct((P * SUBLANE, LANE), F32),
        compiler_params=_params(("parallel", "arbitrary")),
        name="moe_experts",
    )(tile_expert, tile_rows, x_sorted, wg, wu, wd)


def _combine_kernel(s0_ref, s1_ref, e_hbm, h_ref, tw_ref, out_ref, b0, b1, sem, *, rows):
    def issue(r, c):
        _row_copy(e_hbm, b0, sem.at[0], s0_ref[0, 0, r], r).start()
        _row_copy(e_hbm, b1, sem.at[1], s1_ref[0, 0, r], r).start()
        return c

    lax.fori_loop(0, rows, issue, 0, unroll=8)

    def drain(r, c):
        _row_copy(e_hbm, b0, sem.at[0], 0, r).wait()
        _row_copy(e_hbm, b1, sem.at[1], 0, r).wait()
        return c

    lax.fori_loop(0, rows, drain, 0, unroll=8)
    tw = tw_ref[...]
    for s in range(D_MODEL // LANE):
        cs = slice(s * LANE, (s + 1) * LANE)
        out_ref[:, cs] = (h_ref[:, cs] + tw[:, 0:1] * _from_row_tiles(b0, rows, s)
                          + tw[:, 1:2] * _from_row_tiles(b1, rows, s))


def _combine(e_sorted, slot0, slot1, h, topw, rows):
    T = h.shape[0]
    n = T // rows
    smem = lambda: pl.BlockSpec((1, 1, rows), lambda i: (i, 0, 0), memory_space=pltpu.SMEM)
    return pl.pallas_call(
        functools.partial(_combine_kernel, rows=rows),
        grid=(n,),
        in_specs=[smem(), smem(), pl.BlockSpec(memory_space=pl.ANY),
                  pl.BlockSpec((rows, D_MODEL), lambda i: (i, 0)),
                  pl.BlockSpec((rows, LANE), lambda i: (i, 0))],
        out_specs=pl.BlockSpec((rows, D_MODEL), lambda i: (i, 0)),
        out_shape=jax.ShapeDtypeStruct((T, D_MODEL), F32),
        scratch_shapes=[pltpu.VMEM((rows * SUBLANE, LANE), F32),
                        pltpu.VMEM((rows * SUBLANE, LANE), F32),
                        pltpu.SemaphoreType.DMA((2,))],
        compiler_params=_params(("arbitrary",)),
        name="moe_combine",
    )(slot0.reshape(n, 1, rows), slot1.reshape(n, 1, rows), e_sorted, h, topw)


def _route(topi, tm):
    T = topi.shape[0]
    e = topi[:, :2].reshape(-1)
    onehot = (e[:, None] == jnp.arange(N_EXPERTS, dtype=jnp.int32)[None, :]).astype(jnp.int32)
    csum = jnp.cumsum(onehot, axis=0)
    pos = jnp.sum((csum - onehot) * onehot, axis=1)
    counts = csum[-1]
    tiles = (counts + tm - 1) // tm
    tile_end = jnp.cumsum(tiles)
    offs = (tile_end - tiles) * tm
    slot = jnp.sum(onehot * offs[None, :], axis=1) + pos
    n_tiles = 2 * T // tm + N_EXPERTS
    tile_ids = jnp.arange(n_tiles, dtype=jnp.int32)
    tile_expert = jnp.sum((tile_ids[:, None] >= tile_end[None, :]).astype(jnp.int32), axis=1)
    tile_expert = jnp.minimum(tile_expert, N_EXPERTS - 1).astype(jnp.int32)
    first_tile = (tile_end - tiles)[tile_expert]
    tile_rows = jnp.clip(counts[tile_expert] - (tile_ids - first_tile) * tm, 0, tm)
    tile_rows = jnp.where(tile_ids < tile_end[-1], tile_rows, 0).astype(jnp.int32)
    slot2 = slot.reshape(T, 2)
    return n_tiles * tm, tile_expert, tile_rows, slot2[:, 0], slot2[:, 1]


def _norm_kernel(x_ref, g_ref, o_ref):
    x = x_ref[...]
    ms = jnp.mean(x * x, axis=-1, keepdims=True)
    o_ref[...] = x * lax.rsqrt(ms + EPS) * g_ref[...]


def _final_norm(h, g, tm):
    T = h.shape[0]
    return pl.pallas_call(
        _norm_kernel,
        grid=(T // tm,),
        in_specs=[pl.BlockSpec((tm, D_MODEL), lambda i: (i, 0)),
                  pl.BlockSpec((1, D_MODEL), lambda i: (0, 0))],
        out_specs=pl.BlockSpec((tm, D_MODEL), lambda i: (i, 0)),
        out_shape=jax.ShapeDtypeStruct((T, D_MODEL), F32),
        compiler_params=_params(("parallel",)),
        name="final_norm",
    )(h, g)


def _pad_cols(w, n):
    return jnp.pad(w, ((0, 0), (0, n - w.shape[1])))


def _prep_w_in(w):
    gates0 = 4 * D_MLSTM
    gi = _pad_cols(w[:, gates0:gates0 + H_MLSTM], LANE)
    gf = _pad_cols(w[:, gates0 + H_MLSTM:gates0 + 2 * H_MLSTM], LANE)
    rest = w[:, gates0 + 2 * H_MLSTM:]
    return jnp.concatenate([w[:, :gates0], rest, gi, gf], axis=1).astype(BF16)


def kernel(x, norm_mix_g, w_in, b_igate, b_fgate, conv_w, conv_b, g_mlstm, g_sb, w_out,
           norm_ffn_g, ffn_w_gate, ffn_w_up, ffn_w_down, w_router, moe_w_gate, moe_w_up,
           moe_w_down, norm_final_g):
    B, S, D = x.shape
    T = B * S
    depth = w_in.shape[0]
    tm = min(512, T)
    ts = min(256, S)
    tq = min(256, S)
    tm_moe = min(1024, T)
    rows = min(1024, T)

    ii = jnp.arange(CHUNK)
    tri = (ii[None, :] <= ii[:, None]).astype(BF16)
    jj = jnp.arange(tq)
    upper = (jj[:, None] > jj[None, :]).astype(BF16)
    usuf = jnp.concatenate([upper, jnp.ones((tq, LANE), BF16)], axis=1)

    f_dense = ffn_w_gate.shape[2]
    f_pad = -(-f_dense // 256) * 256

    h = x.reshape(T, D)
    for layer in range(depth):
        w1 = _prep_w_in(w_in[layer])
        qk, v_m, o_m, q_s, k_s, v_s, gi, gf = _inproj(h, norm_mix_g[layer][None, :], w1, tm)
        bi = _pad_cols(b_igate[layer][None, :], LANE)
        bf = _pad_cols(b_fgate[layer][None, :], LANE)
        h_m = _mlstm(qk, v_m, o_m, gi, gf, conv_w[layer], conv_b[layer][None, :], bi, bf,
                     g_mlstm[layer][None, :], tri, B, S, ts)
        h_s = _sb_attention(q_s, k_s, v_s, g_sb[layer][None, :], usuf, B, S, tq, 4)
        wo = w_out[layer].astype(BF16)
        gffn = norm_ffn_g[layer][None, :]
        j = layer // 2
        if layer % 2 == 0:
            h, yb = _outproj(h_m, h_s, wo, h, gffn, tm)
            wg = _pad_cols(ffn_w_gate[j], f_pad).astype(BF16)
            wu = _pad_cols(ffn_w_up[j], f_pad).astype(BF16)
            wd = jnp.pad(ffn_w_down[j], ((0, f_pad - f_dense), (0, 0))).astype(BF16)
            h = _ffn(yb, h, wg, wu, wd, tm, 256)
        else:
            wr = _pad_cols(w_router[j], LANE)
            wrh = wr.astype(BF16)
            wrl = (wr - wrh.astype(F32)).astype(BF16)
            h, yb, yf, topi, topw = _outproj(h_m, h_s, wo, h, gffn, tm, router=(wrh, wrl))
            n_slots, tile_expert, tile_rows, slot0, slot1 = _route(topi, tm_moe)
            x_sorted = _dispatch(yf, slot0, slot1, n_slots, rows)
            e_sorted = _experts(x_sorted, tile_expert, tile_rows, moe_w_gate, moe_w_up,
                                moe_w_down, j, tm_moe, 512)
            h = _combine(e_sorted, slot0, slot1, h, topw, rows)
    out = _final_norm(h, norm_final_g[None, :], tm)
    return out.reshape(B, S, D)
```

```python
import functools

import jax
import jax.numpy as jnp
from jax import lax
from jax.experimental import pallas as pl
from jax.experimental.pallas import tpu as pltpu

F32 = jnp.float32
BF16 = jnp.bfloat16

D_MODEL = 1024
D_MLSTM = 512
H_MLSTM = 4
DH_MLSTM = 128
D_SB = 512
H_SB = 8
DH_SB = 64
CONV_K = 4
N_EXPERTS = 8
EPS = 1e-6
M_INIT = -1e30
SKEW = 1

LANE = 128
SUBLANE = 8
assert D_MODEL == SUBLANE * LANE
CHUNK = 128
VMEM_LIMIT = 52 * 1024 * 1024

C_QK, C_V, C_O, C_QS, C_KS, C_VS, C_GI, C_GF, C_END = (
    0, 1024, 1536, 2048, 2560, 3072, 3584, 3712, 3840)


def _params(sem, **kw):
    return pltpu.CompilerParams(dimension_semantics=sem, vmem_limit_bytes=VMEM_LIMIT, **kw)


def _sigmoid(x):
    return 1.0 / (1.0 + jnp.exp(-x))


def _split3(x):
    a = x.astype(BF16)
    r = x - a.astype(F32)
    b = r.astype(BF16)
    c = (r - b.astype(F32)).astype(BF16)
    return a, b, c


def _dot(a, b):
    return jnp.dot(a, b, preferred_element_type=F32)


def _inproj_kernel(x_ref, g_ref, w_ref, qk_ref, v_ref, o_ref, qs_ref, ks_ref, vs_ref,
                   gi_ref, gf_ref):
    x = x_ref[...]
    ms = jnp.mean(x * x, axis=-1, keepdims=True)
    xn = (x * lax.rsqrt(ms + EPS) * g_ref[...]).astype(BF16)

    def mm(lo, hi):
        return _dot(xn, w_ref[:, lo:hi])

    qk_ref[...] = mm(C_QK, C_V)
    v_ref[...] = mm(C_V, C_O).astype(BF16)
    o_ref[...] = mm(C_O, C_QS)
    qs_ref[...] = (mm(C_QS, C_KS) * (DH_SB ** -0.5)).astype(BF16)
    ks_ref[...] = mm(C_KS, C_VS).astype(BF16)
    vs_ref[...] = mm(C_VS, C_GI).astype(BF16)
    gi_ref[...] = mm(C_GI, C_GF)
    gf_ref[...] = mm(C_GF, C_END)


def _inproj(h, g, w, tm):
    T = h.shape[0]
    row = lambda n: pl.BlockSpec((tm, n), lambda i: (i, 0))
    const = lambda a: pl.BlockSpec(a.shape, lambda i: (0, 0))
    out_shape = (
        jax.ShapeDtypeStruct((T, 1024), F32),
        jax.ShapeDtypeStruct((T, 512), BF16),
        jax.ShapeDtypeStruct((T, 512), F32),
        jax.ShapeDtypeStruct((T, 512), BF16),
        jax.ShapeDtypeStruct((T, 512), BF16),
        jax.ShapeDtypeStruct((T, 512), BF16),
        jax.ShapeDtypeStruct((T, LANE), F32),
        jax.ShapeDtypeStruct((T, LANE), F32),
    )
    return pl.pallas_call(
        _inproj_kernel,
        grid=(T // tm,),
        in_specs=[row(1024), const(g), const(w)],
        out_specs=(row(1024), row(512), row(512), row(512), row(512), row(512),
                   row(LANE), row(LANE)),
        out_shape=out_shape,
        compiler_params=_params(("parallel",)),
        name="inproj",
    )(h, g, w)


def _mlstm_kernel(qk_ref, v_ref, o_ref, gi_ref, gf_ref, cw_ref, cb_ref, bi_ref, bf_ref,
                  gm_ref, tri_ref, out_ref, xpad, cext, mst, *, ts):
    s_idx = pl.program_id(1)

    @pl.when(s_idx == 0)
    def _():
        xpad[0:SUBLANE, :] = jnp.zeros((SUBLANE, 2 * D_MLSTM), F32)
        cext[...] = jnp.zeros(cext.shape, F32)
        mst[...] = jnp.full(mst.shape, M_INIT, F32)

    xpad[SUBLANE:SUBLANE + ts, :] = qk_ref[...]
    y = cb_ref[...]
    for tap in range(CONV_K):
        off = SUBLANE - (CONV_K - 1) + tap
        y = y + xpad[off:off + ts, :] * cw_ref[tap:tap + 1, :]
    xpad[0:SUBLANE, :] = xpad[ts:ts + SUBLANE, :]
    act = y * _sigmoid(y)
    q_all = act[:, :D_MLSTM].astype(BF16)
    kt_all = (act[:, D_MLSTM:] * (DH_MLSTM ** -0.5)).T

    row = lax.broadcasted_iota(jnp.int32, (CHUNK, CHUNK), 0)
    col = lax.broadcasted_iota(jnp.int32, (CHUNK, CHUNK), 1)
    causal = col <= row
    ones_blk = jnp.ones((CHUNK, DH_MLSTM), BF16)
    tri = tri_ref[...]

    for c in range(ts // CHUNK):
        r0 = c * CHUNK
        gi = gi_ref[r0:r0 + CHUNK, :] + bi_ref[...]
        gf = gf_ref[r0:r0 + CHUNK, :] + bf_ref[...]
        lf = jnp.minimum(gf, 0.0) - jnp.log(1.0 + jnp.exp(-jnp.abs(gf)))
        l1, l2, l3 = _split3(lf)
        bcum = _dot(tri, l1) + _dot(tri, l2) + _dot(tri, l3)
        a_all = gi - bcum
        a_t = a_all.T
        for h in range(H_MLSTM):
            hs = slice(h * DH_MLSTM, (h + 1) * DH_MLSTM)
            a_row = a_t[h:h + 1, :]
            m_prev = mst[h:h + 1, 0:1]
            mx = jnp.max(jnp.where(causal, a_row, -jnp.inf), axis=-1, keepdims=True)
            big_m = jnp.maximum(m_prev, mx)
            w = jnp.where(causal, jnp.exp(a_row - big_m), 0.0)
            a_inter = jnp.exp(m_prev - big_m)
            m_t = bcum[:, h:h + 1] + big_m
            qh = q_all[r0:r0 + CHUNK, hs]
            kt = kt_all[hs, r0:r0 + CHUNK]
            sb = (_dot(qh, kt.astype(BF16)) * w).astype(BF16)
            vext = jnp.concatenate([v_ref[r0:r0 + CHUNK, hs], ones_blk], axis=1)
            ce = cext[h]
            numext = a_inter * _dot(qh, ce.astype(BF16)) + _dot(sb, vext)
            num = numext[:, :DH_MLSTM]
            den = numext[:, DH_MLSTM:]
            hh = num / jnp.maximum(jnp.abs(den), jnp.exp(-m_t))
            ms = jnp.mean(hh * hh, axis=-1, keepdims=True)
            yh = hh * lax.rsqrt(ms + EPS) * gm_ref[:, hs]
            out_ref[r0:r0 + CHUNK, hs] = (_sigmoid(o_ref[r0:r0 + CHUNK, hs]) * yh).astype(BF16)
            m_last = big_m[CHUNK - 1:CHUNK, :]
            wkt = (kt * jnp.exp(a_row - m_last)).astype(BF16)
            cext[h] = jnp.exp(m_prev - m_last) * ce + _dot(wkt, vext)
            mst[h:h + 1, :] = jnp.broadcast_to(m_t[CHUNK - 1:CHUNK, :], (1, LANE))


def _mlstm(qk, v, o, gi, gf, conv_w, conv_b, bi, bf, gm, tri, B, S, ts):
    T = B * S
    nsb = S // ts
    row = lambda n: pl.BlockSpec((ts, n), lambda b, s: (b * nsb + s, 0))
    const = lambda a: pl.BlockSpec(a.shape, lambda b, s: (0, 0))
    return pl.pallas_call(
        functools.partial(_mlstm_kernel, ts=ts),
        grid=(B, nsb),
        in_specs=[row(1024), row(512), row(512), row(LANE), row(LANE),
                  const(conv_w), const(conv_b), const(bi), const(bf), const(gm), const(tri)],
        out_specs=row(512),
        out_shape=jax.ShapeDtypeStruct((T, D_MLSTM), BF16),
        scratch_shapes=[
            pltpu.VMEM((ts + SUBLANE, 2 * D_MLSTM), F32),
            pltpu.VMEM((H_MLSTM, DH_MLSTM, 2 * DH_MLSTM), F32),
            pltpu.VMEM((SUBLANE, LANE), F32),
        ],
        compiler_params=_params(("parallel", "arbitrary")),
        name="mlstm",
    )(qk, v, o, gi, gf, conv_w, conv_b, bi, bf, gm, tri)


def _sb_kernel(q_ref, k_ref, v_ref, g_ref, u_ref, out_ref, r_scr, acc_scr, *, tq, npb):
    qi = pl.program_id(2)
    lane = lax.broadcasted_iota(jnp.int32, (tq, LANE), 1)
    first = lane < DH_SB
    r_scr[...] = jnp.zeros(r_scr.shape, F32)
    acc_scr[...] = jnp.zeros(acc_scr.shape, F32)

    def head_q(p, e):
        q = q_ref[:, p * LANE:(p + 1) * LANE]
        zero = jnp.zeros_like(q)
        return jnp.where(first, q, zero) if e == 0 else jnp.where(first, zero, q)

    def sweep(g, diag):
        st = pl.multiple_of(g * tq, tq)
        heads = [(p, e) for p in range(npb) for e in range(2)]
        if diag:
            strict = (lax.broadcasted_iota(jnp.int32, (tq, tq), 1)
                      < lax.broadcasted_iota(jnp.int32, (tq, tq), 0))
        n = len(heads)
        zs, lbs, xs = [None] * n, [None] * n, [None] * n

        def scores(i):
            p, e = heads[i]
            zs[i] = lax.dot_general(head_q(p, e), k_ref[pl.ds(st, tq), p * LANE:(p + 1) * LANE],
                                    (((1,), (1,)), ((), ())), preferred_element_type=F32)

        def logs(i):
            z = zs[i]
            sp = jnp.log(1.0 + jnp.exp(-jnp.abs(z)))
            lb = jnp.minimum(z, 0.0) - sp
            lr = lb - z
            if diag:
                lr = jnp.where(strict, lr, 0.0)
            lbs[i] = lb
            xs[i] = _dot(lr.astype(BF16), u_ref[...])

        def weights(i):
            p, e = heads[i]
            r = r_scr[i]
            a = jnp.exp(lbs[i] + xs[i][:, :tq] + jnp.concatenate([r] * (tq // LANE), axis=1))
            if diag:
                a = jnp.where(strict, a, 0.0)
            r_scr[i] = r + xs[i][:, tq:]
            acc_scr[i] += _dot(a.astype(BF16), v_ref[pl.ds(st, tq), p * LANE:(p + 1) * LANE])

        for s in range(n + 2 * SKEW):
            if s < n:
                scores(s)
            if 0 <= s - SKEW < n:
                logs(s - SKEW)
            if 0 <= s - 2 * SKEW < n:
                weights(s - 2 * SKEW)

    sweep(qi, True)

    def body(j, c):
        sweep(qi - 1 - j, False)
        return c

    lax.fori_loop(0, qi, body, 0)

    for p in range(npb):
        o = jnp.where(first, acc_scr[2 * p], acc_scr[2 * p + 1])
        sq = o * o
        s0 = jnp.sum(jnp.where(first, sq, 0.0), axis=-1, keepdims=True)
        s1 = jnp.sum(jnp.where(first, 0.0, sq), axis=-1, keepdims=True)
        ms = jnp.where(first, s0, s1) * (1.0 / DH_SB)
        ls = slice(p * LANE, (p + 1) * LANE)
        out_ref[:, ls] = (o * lax.rsqrt(ms + EPS) * g_ref[:, ls]).astype(BF16)


def _sb_attention(qs, ks, vs, g_sb, usuf, B, S, tq, npb):
    T = B * S
    nq = S // tq
    w = npb * LANE
    return pl.pallas_call(
        functools.partial(_sb_kernel, tq=tq, npb=npb),
        grid=(B, D_SB // w, nq),
        in_specs=[
            pl.BlockSpec((tq, w), lambda b, p, i: (b * nq + i, p)),
            pl.BlockSpec((S, w), lambda b, p, i: (b, p)),
            pl.BlockSpec((S, w), lambda b, p, i: (b, p)),
            pl.BlockSpec((1, w), lambda b, p, i: (0, p)),
            pl.BlockSpec(usuf.shape, lambda b, p, i: (0, 0)),
        ],
        out_specs=pl.BlockSpec((tq, w), lambda b, p, i: (b * nq + i, p)),
        out_shape=jax.ShapeDtypeStruct((T, D_SB), BF16),
        scratch_shapes=[pltpu.VMEM((2 * npb, tq, LANE), F32),
                        pltpu.VMEM((2 * npb, tq, LANE), F32)],
        compiler_params=_params(("parallel", "parallel", "arbitrary")),
        name="sb_attention",
    )(qs, ks, vs, g_sb, usuf)


def _outproj_kernel(hm_ref, hs_ref, w_ref, h_ref, g_ref, *rest, moe):
    if moe:
        wrh_ref, wrl_ref, hn_ref, yb_ref, yf_ref, ti_ref, tw_ref = rest
    else:
        hn_ref, yb_ref = rest
    hn = h_ref[...] + _dot(hm_ref[...], w_ref[0:D_MLSTM, :]) + _dot(hs_ref[...], w_ref[D_MLSTM:, :])
    hn_ref[...] = hn
    ms = jnp.mean(hn * hn, axis=-1, keepdims=True)
    y = hn * lax.rsqrt(ms + EPS) * g_ref[...]
    yb = y.astype(BF16)
    yb_ref[...] = yb
    if moe:
        _to_row_tiles(yf_ref, yb.astype(F32))
        yl = (y - yb.astype(F32)).astype(BF16)
        logits = _dot(yb, wrh_ref[...]) + _dot(yl, wrh_ref[...]) + _dot(yb, wrl_ref[...])
        lane = lax.broadcasted_iota(jnp.int32, logits.shape, 1)
        lanef = lane.astype(F32)
        lg = jnp.where(lane < N_EXPERTS, logits, -jnp.inf)
        m1 = jnp.max(lg, axis=-1, keepdims=True)
        i1 = jnp.min(jnp.where(lg == m1, lanef, float(LANE)), axis=-1, keepdims=True)
        lg2 = jnp.where(lanef == i1, -jnp.inf, lg)
        m2 = jnp.max(lg2, axis=-1, keepdims=True)
        i2 = jnp.min(jnp.where(lg2 == m2, lanef, float(LANE)), axis=-1, keepdims=True)
        t = jnp.exp(m2 - m1)
        w1 = 1.0 / (1.0 + t)
        w2 = t * w1
        ti_ref[...] = jnp.where(lane == 0, i1, jnp.where(lane == 1, i2, 0.0)).astype(jnp.int32)
        tw_ref[...] = jnp.where(lane == 0, w1, jnp.where(lane == 1, w2, 0.0))


def _outproj(hm, hs, w_out, h, g, tm, router=None):
    T = h.shape[0]
    moe = router is not None
    row = lambda n: pl.BlockSpec((tm, n), lambda i: (i, 0))
    const = lambda a: pl.BlockSpec(a.shape, lambda i: (0, 0))
    ins = [hm, hs, w_out, h, g]
    in_specs = [row(512), row(512), const(w_out), row(1024), const(g)]
    out_shape = [jax.ShapeDtypeStruct((T, D_MODEL), F32), jax.ShapeDtypeStruct((T, D_MODEL), BF16)]
    out_specs = [row(1024), row(1024)]
    if moe:
        ins += list(router)
        in_specs += [const(router[0]), const(router[1])]
        out_shape += [jax.ShapeDtypeStruct((T * SUBLANE, LANE), F32),
                      jax.ShapeDtypeStruct((T, LANE), jnp.int32),
                      jax.ShapeDtypeStruct((T, LANE), F32)]
        out_specs += [pl.BlockSpec((tm * SUBLANE, LANE), lambda i: (i, 0)), row(LANE), row(LANE)]
    return pl.pallas_call(
        functools.partial(_outproj_kernel, moe=moe),
        grid=(T // tm,),
        in_specs=in_specs,
        out_specs=tuple(out_specs),
        out_shape=tuple(out_shape),
        compiler_params=_params(("parallel",)),
        name="outproj_moe" if moe else "outproj",
    )(*ins)


def _ffn_kernel(y_ref, h_ref, wg_ref, wu_ref, wd_ref, out_ref, *, fc):
    y = y_ref[...]
    out_ref[...] = h_ref[...]
    for c in range(wg_ref.shape[1] // fc):
        cs = slice(c * fc, (c + 1) * fc)
        g = _dot(y, wg_ref[:, cs])
        u = _dot(y, wu_ref[:, cs])
        a = (g * _sigmoid(g) * u).astype(BF16)
        out_ref[...] += _dot(a, wd_ref[cs, :])


def _ffn(y, h, wg, wu, wd, tm, fc):
    T = h.shape[0]
    row = lambda n: pl.BlockSpec((tm, n), lambda i: (i, 0))
    const = lambda a: pl.BlockSpec(a.shape, lambda i: (0, 0))
    return pl.pallas_call(
        functools.partial(_ffn_kernel, fc=fc),
        grid=(T // tm,),
        in_specs=[row(1024), row(1024), const(wg), const(wu), const(wd)],
        out_specs=row(1024),
        out_shape=jax.ShapeDtypeStruct((T, D_MODEL), F32),
        compiler_params=_params(("parallel",)),
        name="ffn",
    )(y, h, wg, wu, wd)


def _row_copy(src, dst, sem, s, d):
    return pltpu.make_async_copy(
        src.at[pl.ds(pl.multiple_of(s * SUBLANE, SUBLANE), SUBLANE), :],
        dst.at[pl.ds(pl.multiple_of(d * SUBLANE, SUBLANE), SUBLANE), :], sem)


def _to_row_tiles(ref, x):
    n = x.shape[0]
    for s in range(D_MODEL // LANE):
        ref[pl.ds(s, n, stride=SUBLANE), :] = x[:, s * LANE:(s + 1) * LANE]


def _from_row_tiles(ref, n, s):
    return ref[pl.ds(s, n, stride=SUBLANE), :]


def _dispatch_kernel(s0_ref, s1_ref, y_ref, zero_hbm, x_hbm, sem, *, rows):
    del zero_hbm

    def issue(k, c):
        base = pl.multiple_of(k * SUBLANE, SUBLANE)
        for u in range(SUBLANE):
            _row_copy(y_ref, x_hbm, sem.at[0], base + u, s0_ref[0, 0, base + u]).start()
            _row_copy(y_ref, x_hbm, sem.at[1], base + u, s1_ref[0, 0, base + u]).start()
        return c

    lax.fori_loop(0, rows // SUBLANE, issue, 0)

    def drain(r, c):
        _row_copy(y_ref, x_hbm, sem.at[0], 0, 0).wait()
        _row_copy(y_ref, x_hbm, sem.at[1], 0, 0).wait()
        return c

    lax.fori_loop(0, rows, drain, 0, unroll=8)


def _dispatch(y_tiles, slot0, slot1, n_slots, rows):
    T = slot0.shape[0]
    n = T // rows
    smem = lambda: pl.BlockSpec((1, 1, rows), lambda i: (i, 0, 0), memory_space=pltpu.SMEM)
    zeros = jnp.zeros((n_slots * SUBLANE, LANE), F32)
    return pl.pallas_call(
        functools.partial(_dispatch_kernel, rows=rows),
        grid=(n,),
        in_specs=[smem(), smem(), pl.BlockSpec((rows * SUBLANE, LANE), lambda i: (i, 0)),
                  pl.BlockSpec(memory_space=pl.ANY)],
        out_specs=pl.BlockSpec(memory_space=pl.ANY),
        out_shape=jax.ShapeDtypeStruct((n_slots * SUBLANE, LANE), F32),
        input_output_aliases={3: 0},
        scratch_shapes=[pltpu.SemaphoreType.DMA((2,))],
        compiler_params=_params(("arbitrary",)),
        name="moe_dispatch",
    )(slot0.reshape(n, 1, rows), slot1.reshape(n, 1, rows), y_tiles, zeros)


def _expert_kernel(te_ref, tr_ref, x_ref, wg_ref, wu_ref, wd_ref, out_ref, xb, acc, *, tm):
    i = pl.program_id(0)
    f = pl.program_id(1)

    @pl.when(f == 0)
    def _():
        for s in range(D_MODEL // LANE):
            xb[:, s * LANE:(s + 1) * LANE] = _from_row_tiles(x_ref, tm, s).astype(BF16)
        acc[...] = jnp.zeros(acc.shape, F32)

    def swiglu(nrows):
        x = xb[0:nrows, :]
        g = _dot(x, wg_ref[...].astype(BF16))
        u = _dot(x, wu_ref[...].astype(BF16))
        a = (g * _sigmoid(g) * u).astype(BF16)
        acc[0:nrows, :] += _dot(a, wd_ref[...].astype(BF16))

    valid = tr_ref[i]

    @pl.when(valid > tm // 2)
    def _():
        swiglu(tm)

    @pl.when((valid > 0) & (valid <= tm // 2))
    def _():
        swiglu(tm // 2)

    @pl.when(f == pl.num_programs(1) - 1)
    def _():
        _to_row_tiles(out_ref, acc[...])


def _experts(x_sorted, tile_expert, tile_rows, wg, wu, wd, layer, tm, tf):
    P = x_sorted.shape[0] // SUBLANE
    n = P // tm
    F = wg.shape[3]
    grid_spec = pltpu.PrefetchScalarGridSpec(
        num_scalar_prefetch=2,
        grid=(n, F // tf),
        in_specs=[
            pl.BlockSpec((tm * SUBLANE, LANE), lambda i, f, te, tr: (i, 0)),
            pl.BlockSpec((None, None, D_MODEL, tf), lambda i, f, te, tr: (layer, te[i], 0, f)),
            pl.BlockSpec((None, None, D_MODEL, tf), lambda i, f, te, tr: (layer, te[i], 0, f)),
            pl.BlockSpec((None, None, tf, D_MODEL), lambda i, f, te, tr: (layer, te[i], f, 0)),
        ],
        out_specs=pl.BlockSpec((tm * SUBLANE, LANE), lambda i, f, te, tr: (i, 0)),
        scratch_shapes=[pltpu.VMEM((tm, D_MODEL), BF16), pltpu.VMEM((tm, D_MODEL), F32)],
    )
    return pl.pallas_call(
        functools.partial(_expert_kernel, tm=tm),
        grid_spec=grid_spec,
        out_shape=jax.ShapeDtypeStruct((P * SUBLANE, LANE), F32),
        compiler_params=_params(("parallel", "arbitrary")),
        name="moe_experts",
    )(tile_expert, tile_rows, x_sorted, wg, wu, wd)


def _combine_kernel(s0_ref, s1_ref, e_hbm, h_ref, tw_ref, *rest, rows, final):
    if final:
        gfin_ref, out_ref, b0, b1, sem = rest
    else:
        out_ref, b0, b1, sem = rest

    def issue(r, c):
        _row_copy(e_hbm, b0, sem.at[0], s0_ref[0, 0, r], r).start()
        _row_copy(e_hbm, b1, sem.at[1], s1_ref[0, 0, r], r).start()
        return c

    lax.fori_loop(0, rows, issue, 0, unroll=8)

    def drain(r, c):
        _row_copy(e_hbm, b0, sem.at[0], 0, r).wait()
        _row_copy(e_hbm, b1, sem.at[1], 0, r).wait()
        return c

    lax.fori_loop(0, rows, drain, 0, unroll=8)
    tw = tw_ref[...]
    for s in range(D_MODEL // LANE):
        cs = slice(s * LANE, (s + 1) * LANE)
        out_ref[:, cs] = (h_ref[:, cs] + tw[:, 0:1] * _from_row_tiles(b0, rows, s)
                          + tw[:, 1:2] * _from_row_tiles(b1, rows, s))
    if final:
        x = out_ref[...]
        ms = jnp.mean(x * x, axis=-1, keepdims=True)
        out_ref[...] = x * lax.rsqrt(ms + EPS) * gfin_ref[...]


def _combine(e_sorted, slot0, slot1, h, topw, rows, g_final=None):
    T = h.shape[0]
    n = T // rows
    final = g_final is not None
    smem = lambda: pl.BlockSpec((1, 1, rows), lambda i: (i, 0, 0), memory_space=pltpu.SMEM)
    ins = [slot0.reshape(n, 1, rows), slot1.reshape(n, 1, rows), e_sorted, h, topw]
    in_specs = [smem(), smem(), pl.BlockSpec(memory_space=pl.ANY),
                pl.BlockSpec((rows, D_MODEL), lambda i: (i, 0)),
                pl.BlockSpec((rows, LANE), lambda i: (i, 0))]
    if final:
        ins.append(g_final)
        in_specs.append(pl.BlockSpec((1, D_MODEL), lambda i: (0, 0)))
    return pl.pallas_call(
        functools.partial(_combine_kernel, rows=rows, final=final),
        grid=(n,),
        in_specs=in_specs,
        out_specs=pl.BlockSpec((rows, D_MODEL), lambda i: (i, 0)),
        out_shape=jax.ShapeDtypeStruct((T, D_MODEL), F32),
        scratch_shapes=[pltpu.VMEM((rows * SUBLANE, LANE), F32),
                        pltpu.VMEM((rows * SUBLANE, LANE), F32),
                        pltpu.SemaphoreType.DMA((2,))],
        compiler_params=_params(("arbitrary",)),
        name="moe_combine_final" if final else "moe_combine",
    )(*ins)


def _route(topi, tm):
    T = topi.shape[0]
    e = topi[:, :2].reshape(-1)
    onehot = (e[:, None] == jnp.arange(N_EXPERTS, dtype=jnp.int32)[None, :]).astype(jnp.int32)
    csum = jnp.cumsum(onehot, axis=0)
    pos = jnp.sum((csum - onehot) * onehot, axis=1)
    counts = csum[-1]
    tiles = (counts + tm - 1) // tm
    tile_end = jnp.cumsum(tiles)
    offs = (tile_end - tiles) * tm
    slot = jnp.sum(onehot * offs[None, :], axis=1) + pos
    n_tiles = 2 * T // tm + N_EXPERTS
    tile_ids = jnp.arange(n_tiles, dtype=jnp.int32)
    tile_expert = jnp.sum((tile_ids[:, None] >= tile_end[None, :]).astype(jnp.int32), axis=1)
    tile_expert = jnp.minimum(tile_expert, N_EXPERTS - 1).astype(jnp.int32)
    first_tile = (tile_end - tiles)[tile_expert]
    tile_rows = jnp.clip(counts[tile_expert] - (tile_ids - first_tile) * tm, 0, tm)
    tile_rows = jnp.where(tile_ids < tile_end[-1], tile_rows, 0).astype(jnp.int32)
    slot2 = slot.reshape(T, 2)
    return n_tiles * tm, tile_expert, tile_rows, slot2[:, 0], slot2[:, 1]


def _norm_kernel(x_ref, g_ref, o_ref):
    x = x_ref[...]
    ms = jnp.mean(x * x, axis=-1, keepdims=True)
    o_ref[...] = x * lax.rsqrt(ms + EPS) * g_ref[...]


def _final_norm(h, g, tm):
    T = h.shape[0]
    return pl.pallas_call(
        _norm_kernel,
        grid=(T // tm,),
        in_specs=[pl.BlockSpec((tm, D_MODEL), lambda i: (i, 0)),
                  pl.BlockSpec((1, D_MODEL), lambda i: (0, 0))],
        out_specs=pl.BlockSpec((tm, D_MODEL), lambda i: (i, 0)),
        out_shape=jax.ShapeDtypeStruct((T, D_MODEL), F32),
        compiler_params=_params(("parallel",)),
        name="final_norm",
    )(h, g)


def _pad_cols(w, n):
    return jnp.pad(w, ((0, 0), (0, n - w.shape[1])))


def _prep_w_in(w):
    gates0 = 4 * D_MLSTM
    gi = _pad_cols(w[:, gates0:gates0 + H_MLSTM], LANE)
    gf = _pad_cols(w[:, gates0 + H_MLSTM:gates0 + 2 * H_MLSTM], LANE)
    rest = w[:, gates0 + 2 * H_MLSTM:]
    return jnp.concatenate([w[:, :gates0], rest, gi, gf], axis=1).astype(BF16)


def kernel(x, norm_mix_g, w_in, b_igate, b_fgate, conv_w, conv_b, g_mlstm, g_sb, w_out,
           norm_ffn_g, ffn_w_gate, ffn_w_up, ffn_w_down, w_router, moe_w_gate, moe_w_up,
           moe_w_down, norm_final_g):
    B, S, D = x.shape
    T = B * S
    depth = w_in.shape[0]
    tm = min(512, T)
    ts = min(256, S)
    tq = min(256, S)
    tm_moe = min(1024, T)
    rows = min(1024, T)

    ii = jnp.arange(CHUNK)
    tri = (ii[None, :] <= ii[:, None]).astype(BF16)
    jj = jnp.arange(tq)
    upper = (jj[:, None] > jj[None, :]).astype(BF16)
    usuf = jnp.concatenate([upper, jnp.ones((tq, LANE), BF16)], axis=1)

    f_dense = ffn_w_gate.shape[2]
    f_pad = -(-f_dense // 256) * 256

    h = x.reshape(T, D)
    for layer in range(depth):
        w1 = _prep_w_in(w_in[layer])
        qk, v_m, o_m, q_s, k_s, v_s, gi, gf = _inproj(h, norm_mix_g[layer][None, :], w1, tm)
        bi = _pad_cols(b_igate[layer][None, :], LANE)
        bf = _pad_cols(b_fgate[layer][None, :], LANE)
        h_m = _mlstm(qk, v_m, o_m, gi, gf, conv_w[layer], conv_b[layer][None, :], bi, bf,
                     g_mlstm[layer][None, :], tri, B, S, ts)
        h_s = _sb_attention(q_s, k_s, v_s, g_sb[layer][None, :], usuf, B, S, tq, 4)
        wo = w_out[layer].astype(BF16)
        gffn = norm_ffn_g[layer][None, :]
        j = layer // 2
        if layer % 2 == 0:
            h, yb = _outproj(h_m, h_s, wo, h, gffn, tm)
            wg = _pad_cols(ffn_w_gate[j], f_pad).astype(BF16)
            wu = _pad_cols(ffn_w_up[j], f_pad).astype(BF16)
            wd = jnp.pad(ffn_w_down[j], ((0, f_pad - f_dense), (0, 0))).astype(BF16)
            h = _ffn(yb, h, wg, wu, wd, tm, 256)
        else:
            wr = _pad_cols(w_router[j], LANE)
            wrh = wr.astype(BF16)
            wrl = (wr - wrh.astype(F32)).astype(BF16)
            h, yb, yf, topi, topw = _outproj(h_m, h_s, wo, h, gffn, tm, router=(wrh, wrl))
            n_slots, tile_expert, tile_rows, slot0, slot1 = _route(topi, tm_moe)
            x_sorted = _dispatch(yf, slot0, slot1, n_slots, rows)
            e_sorted = _experts(x_sorted, tile_expert, tile_rows, moe_w_gate, moe_w_up,
                                moe_w_down, j, tm_moe, 512)
            last = layer == depth - 1
            h = _combine(e_sorted, slot0, slot1, h, topw, rows,
                         g_final=norm_final_g[None, :] if last else None)
    out = h if depth % 2 == 0 else _final_norm(h, norm_final_g[None, :], tm)
    return out.reshape(B, S, D)
```

```python
import functools

import jax
import jax.numpy as jnp
from jax import lax
from jax.experimental import pallas as pl
from jax.experimental.pallas import tpu as pltpu

F32 = jnp.float32
BF16 = jnp.bfloat16

D_MODEL = 1024
D_MLSTM = 512
H_MLSTM = 4
DH_MLSTM = 128
D_SB = 512
H_SB = 8
DH_SB = 64
CONV_K = 4
N_EXPERTS = 8
EPS = 1e-6
M_INIT = -1e30
SKEW = 1

LANE = 128
SUBLANE = 8
assert D_MODEL == SUBLANE * LANE
CHUNK = 128
VMEM_LIMIT = 52 * 1024 * 1024

C_QK, C_V, C_O, C_QS, C_KS, C_VS, C_GI, C_GF, C_END = (
    0, 1024, 1536, 2048, 2560, 3072, 3584, 3712, 3840)


def _params(sem, **kw):
    return pltpu.CompilerParams(dimension_semantics=sem, vmem_limit_bytes=VMEM_LIMIT, **kw)


def _sigmoid(x):
    return 1.0 / (1.0 + jnp.exp(-x))


def _split3(x):
    a = x.astype(BF16)
    r = x - a.astype(F32)
    b = r.astype(BF16)
    c = (r - b.astype(F32)).astype(BF16)
    return a, b, c


def _dot(a, b):
    return jnp.dot(a, b, preferred_element_type=F32)


def _inproj_kernel(x_ref, g_ref, w_ref, qk_ref, v_ref, o_ref, qs_ref, ks_ref, vs_ref,
                   gi_ref, gf_ref):
    x = x_ref[...]
    ms = jnp.mean(x * x, axis=-1, keepdims=True)
    xn = (x * lax.rsqrt(ms + EPS) * g_ref[...]).astype(BF16)

    def mm(lo, hi):
        return _dot(xn, w_ref[:, lo:hi])

    qk_ref[...] = mm(C_QK, C_V)
    v_ref[...] = mm(C_V, C_O).astype(BF16)
    o_ref[...] = mm(C_O, C_QS)
    qs_ref[...] = (mm(C_QS, C_KS) * (DH_SB ** -0.5)).astype(BF16)
    ks_ref[...] = mm(C_KS, C_VS).astype(BF16)
    vs_ref[...] = mm(C_VS, C_GI).astype(BF16)
    gi_ref[...] = mm(C_GI, C_GF)
    gf_ref[...] = mm(C_GF, C_END)


def _inproj(h, g, w, tm):
    T = h.shape[0]
    row = lambda n: pl.BlockSpec((tm, n), lambda i: (i, 0))
    const = lambda a: pl.BlockSpec(a.shape, lambda i: (0, 0))
    out_shape = (
        jax.ShapeDtypeStruct((T, 1024), F32),
        jax.ShapeDtypeStruct((T, 512), BF16),
        jax.ShapeDtypeStruct((T, 512), F32),
        jax.ShapeDtypeStruct((T, 512), BF16),
        jax.ShapeDtypeStruct((T, 512), BF16),
        jax.ShapeDtypeStruct((T, 512), BF16),
        jax.ShapeDtypeStruct((T, LANE), F32),
        jax.ShapeDtypeStruct((T, LANE), F32),
    )
    return pl.pallas_call(
        _inproj_kernel,
        grid=(T // tm,),
        in_specs=[row(1024), const(g), const(w)],
        out_specs=(row(1024), row(512), row(512), row(512), row(512), row(512),
                   row(LANE), row(LANE)),
        out_shape=out_shape,
        compiler_params=_params(("parallel",)),
        name="inproj",
    )(h, g, w)


def _mlstm_kernel(qk_ref, v_ref, o_ref, gi_ref, gf_ref, cw_ref, cb_ref, bi_ref, bf_ref,
                  gm_ref, tri_ref, out_ref, xpad, cext, mst, *, ts):
    s_idx = pl.program_id(1)

    @pl.when(s_idx == 0)
    def _():
        xpad[0:SUBLANE, :] = jnp.zeros((SUBLANE, 2 * D_MLSTM), F32)
        cext[...] = jnp.zeros(cext.shape, F32)
        mst[...] = jnp.full(mst.shape, M_INIT, F32)

    xpad[SUBLANE:SUBLANE + ts, :] = qk_ref[...]
    y = cb_ref[...]
    for tap in range(CONV_K):
        off = SUBLANE - (CONV_K - 1) + tap
        y = y + xpad[off:off + ts, :] * cw_ref[tap:tap + 1, :]
    xpad[0:SUBLANE, :] = xpad[ts:ts + SUBLANE, :]
    act = y * _sigmoid(y)
    q_all = act[:, :D_MLSTM].astype(BF16)
    kt_all = (act[:, D_MLSTM:] * (DH_MLSTM ** -0.5)).T

    row = lax.broadcasted_iota(jnp.int32, (CHUNK, CHUNK), 0)
    col = lax.broadcasted_iota(jnp.int32, (CHUNK, CHUNK), 1)
    causal = col <= row
    ones_blk = jnp.ones((CHUNK, DH_MLSTM), BF16)
    tri = tri_ref[...]

    for c in range(ts // CHUNK):
        r0 = c * CHUNK
        gi = gi_ref[r0:r0 + CHUNK, :] + bi_ref[...]
        gf = gf_ref[r0:r0 + CHUNK, :] + bf_ref[...]
        lf = jnp.minimum(gf, 0.0) - jnp.log(1.0 + jnp.exp(-jnp.abs(gf)))
        l1, l2, l3 = _split3(lf)
        bcum = _dot(tri, l1) + _dot(tri, l2) + _dot(tri, l3)
        a_all = gi - bcum
        a_t = a_all.T
        for h in range(H_MLSTM):
            hs = slice(h * DH_MLSTM, (h + 1) * DH_MLSTM)
            a_row = a_t[h:h + 1, :]
            m_prev = mst[h:h + 1, 0:1]
            mx = jnp.max(jnp.where(causal, a_row, -jnp.inf), axis=-1, keepdims=True)
            big_m = jnp.maximum(m_prev, mx)
            w = jnp.where(causal, jnp.exp(a_row - big_m), 0.0)
            a_inter = jnp.exp(m_prev - big_m)
            m_t = bcum[:, h:h + 1] + big_m
            qh = q_all[r0:r0 + CHUNK, hs]
            kt = kt_all[hs, r0:r0 + CHUNK]
            sb = (_dot(qh, kt.astype(BF16)) * w).astype(BF16)
            vext = jnp.concatenate([v_ref[r0:r0 + CHUNK, hs], ones_blk], axis=1)
            ce = cext[h]
            numext = a_inter * _dot(qh, ce.astype(BF16)) + _dot(sb, vext)
            num = numext[:, :DH_MLSTM]
            den = numext[:, DH_MLSTM:]
            hh = num / jnp.maximum(jnp.abs(den), jnp.exp(-m_t))
            ms = jnp.mean(hh * hh, axis=-1, keepdims=True)
            yh = hh * lax.rsqrt(ms + EPS) * gm_ref[:, hs]
            out_ref[r0:r0 + CHUNK, hs] = (_sigmoid(o_ref[r0:r0 + CHUNK, hs]) * yh).astype(BF16)
            m_last = big_m[CHUNK - 1:CHUNK, :]
            wkt = (kt * jnp.exp(a_row - m_last)).astype(BF16)
            cext[h] = jnp.exp(m_prev - m_last) * ce + _dot(wkt, vext)
            mst[h:h + 1, :] = jnp.broadcast_to(m_t[CHUNK - 1:CHUNK, :], (1, LANE))


def _mlstm(qk, v, o, gi, gf, conv_w, conv_b, bi, bf, gm, tri, B, S, ts):
    T = B * S
    nsb = S // ts
    row = lambda n: pl.BlockSpec((ts, n), lambda b, s: (b * nsb + s, 0))
    const = lambda a: pl.BlockSpec(a.shape, lambda b, s: (0, 0))
    return pl.pallas_call(
        functools.partial(_mlstm_kernel, ts=ts),
        grid=(B, nsb),
        in_specs=[row(1024), row(512), row(512), row(LANE), row(LANE),
                  const(conv_w), const(conv_b), const(bi), const(bf), const(gm), const(tri)],
        out_specs=row(512),
        out_shape=jax.ShapeDtypeStruct((T, D_MLSTM), BF16),
        scratch_shapes=[
            pltpu.VMEM((ts + SUBLANE, 2 * D_MLSTM), F32),
            pltpu.VMEM((H_MLSTM, DH_MLSTM, 2 * DH_MLSTM), F32),
            pltpu.VMEM((SUBLANE, LANE), F32),
        ],
        compiler_params=_params(("parallel", "arbitrary")),
        name="mlstm",
    )(qk, v, o, gi, gf, conv_w, conv_b, bi, bf, gm, tri)


def _sb_kernel(q_ref, k_ref, v_ref, g_ref, u_ref, out_ref, r_scr, acc_scr, *, tq, npb):
    qi = pl.program_id(2)
    lane = lax.broadcasted_iota(jnp.int32, (tq, LANE), 1)
    first = lane < DH_SB
    r_scr[...] = jnp.zeros(r_scr.shape, F32)
    acc_scr[...] = jnp.zeros(acc_scr.shape, F32)

    def head_q(p, e):
        q = q_ref[:, p * LANE:(p + 1) * LANE]
        zero = jnp.zeros_like(q)
        return jnp.where(first, q, zero) if e == 0 else jnp.where(first, zero, q)

    def sweep(g, diag):
        st = pl.multiple_of(g * tq, tq)
        heads = [(p, e) for p in range(npb) for e in range(2)]
        if diag:
            strict = (lax.broadcasted_iota(jnp.int32, (tq, tq), 1)
                      < lax.broadcasted_iota(jnp.int32, (tq, tq), 0))
        n = len(heads)
        zs, lbs, xs = [None] * n, [None] * n, [None] * n

        def scores(i):
            p, e = heads[i]
            zs[i] = lax.dot_general(head_q(p, e), k_ref[pl.ds(st, tq), p * LANE:(p + 1) * LANE],
                                    (((1,), (1,)), ((), ())), preferred_element_type=F32)

        def logs(i):
            z = zs[i]
            sp = jnp.log(1.0 + jnp.exp(-jnp.abs(z)))
            lb = jnp.minimum(z, 0.0) - sp
            lr = lb - z
            if diag:
                lr = jnp.where(strict, lr, 0.0)
            lbs[i] = lb
            xs[i] = _dot(lr.astype(BF16), u_ref[...])

        def weights(i):
            p, e = heads[i]
            r = r_scr[i]
            a = jnp.exp(lbs[i] + xs[i][:, :tq] + jnp.concatenate([r] * (tq // LANE), axis=1))
            if diag:
                a = jnp.where(strict, a, 0.0)
            r_scr[i] = r + xs[i][:, tq:]
            acc_scr[i] += _dot(a.astype(BF16), v_ref[pl.ds(st, tq), p * LANE:(p + 1) * LANE])

        for s in range(n + 2 * SKEW):
            if s < n:
                scores(s)
            if 0 <= s - SKEW < n:
                logs(s - SKEW)
            if 0 <= s - 2 * SKEW < n:
                weights(s - 2 * SKEW)

    sweep(qi, True)

    def body(j, c):
        g = qi - 1 - 2 * j
        sweep(g, False)
        sweep(g - 1, False)
        return c

    lax.fori_loop(0, qi // 2, body, 0)

    @pl.when(qi % 2 == 1)
    def _():
        sweep(0, False)

    for p in range(npb):
        o = jnp.where(first, acc_scr[2 * p], acc_scr[2 * p + 1])
        sq = o * o
        s0 = jnp.sum(jnp.where(first, sq, 0.0), axis=-1, keepdims=True)
        s1 = jnp.sum(jnp.where(first, 0.0, sq), axis=-1, keepdims=True)
        ms = jnp.where(first, s0, s1) * (1.0 / DH_SB)
        ls = slice(p * LANE, (p + 1) * LANE)
        out_ref[:, ls] = (o * lax.rsqrt(ms + EPS) * g_ref[:, ls]).astype(BF16)


def _sb_attention(qs, ks, vs, g_sb, usuf, B, S, tq, npb):
    T = B * S
    nq = S // tq
    w = npb * LANE
    return pl.pallas_call(
        functools.partial(_sb_kernel, tq=tq, npb=npb),
        grid=(B, D_SB // w, nq),
        in_specs=[
            pl.BlockSpec((tq, w), lambda b, p, i: (b * nq + i, p)),
            pl.BlockSpec((S, w), lambda b, p, i: (b, p)),
            pl.BlockSpec((S, w), lambda b, p, i: (b, p)),
            pl.BlockSpec((1, w), lambda b, p, i: (0, p)),
            pl.BlockSpec(usuf.shape, lambda b, p, i: (0, 0)),
        ],
        out_specs=pl.BlockSpec((tq, w), lambda b, p, i: (b * nq + i, p)),
        out_shape=jax.ShapeDtypeStruct((T, D_SB), BF16),
        scratch_shapes=[pltpu.VMEM((2 * npb, tq, LANE), F32),
                        pltpu.VMEM((2 * npb, tq, LANE), F32)],
        compiler_params=_params(("parallel", "parallel", "arbitrary")),
        name="sb_attention",
    )(qs, ks, vs, g_sb, usuf)


def _outproj_kernel(hm_ref, hs_ref, w_ref, h_ref, g_ref, *rest, moe):
    if moe:
        wrh_ref, wrl_ref, hn_ref, yb_ref, yf_ref, ti_ref, tw_ref = rest
    else:
        hn_ref, yb_ref = rest
    hn = h_ref[...] + _dot(hm_ref[...], w_ref[0:D_MLSTM, :]) + _dot(hs_ref[...], w_ref[D_MLSTM:, :])
    hn_ref[...] = hn
    ms = jnp.mean(hn * hn, axis=-1, keepdims=True)
    y = hn * lax.rsqrt(ms + EPS) * g_ref[...]
    yb = y.astype(BF16)
    yb_ref[...] = yb
    if moe:
        _to_row_tiles(yf_ref, yb.astype(F32))
        yl = (y - yb.astype(F32)).astype(BF16)
        logits = _dot(yb, wrh_ref[...]) + _dot(yl, wrh_ref[...]) + _dot(yb, wrl_ref[...])
        lane = lax.broadcasted_iota(jnp.int32, logits.shape, 1)
        lanef = lane.astype(F32)
        lg = jnp.where(lane < N_EXPERTS, logits, -jnp.inf)
        m1 = jnp.max(lg, axis=-1, keepdims=True)
        i1 = jnp.min(jnp.where(lg == m1, lanef, float(LANE)), axis=-1, keepdims=True)
        lg2 = jnp.where(lanef == i1, -jnp.inf, lg)
        m2 = jnp.max(lg2, axis=-1, keepdims=True)
        i2 = jnp.min(jnp.where(lg2 == m2, lanef, float(LANE)), axis=-1, keepdims=True)
        t = jnp.exp(m2 - m1)
        w1 = 1.0 / (1.0 + t)
        w2 = t * w1
        ti_ref[...] = jnp.where(lane == 0, i1, jnp.where(lane == 1, i2, 0.0)).astype(jnp.int32)
        tw_ref[...] = jnp.where(lane == 0, w1, jnp.where(lane == 1, w2, 0.0))


def _outproj(hm, hs, w_out, h, g, tm, router=None):
    T = h.shape[0]
    moe = router is not None
    row = lambda n: pl.BlockSpec((tm, n), lambda i: (i, 0))
    const = lambda a: pl.BlockSpec(a.shape, lambda i: (0, 0))
    ins = [hm, hs, w_out, h, g]
    in_specs = [row(512), row(512), const(w_out), row(1024), const(g)]
    out_shape = [jax.ShapeDtypeStruct((T, D_MODEL), F32), jax.ShapeDtypeStruct((T, D_MODEL), BF16)]
    out_specs = [row(1024), row(1024)]
    if moe:
        ins += list(router)
        in_specs += [const(router[0]), const(router[1])]
        out_shape += [jax.ShapeDtypeStruct((T * SUBLANE, LANE), F32),
                      jax.ShapeDtypeStruct((T, LANE), jnp.int32),
                      jax.ShapeDtypeStruct((T, LANE), F32)]
        out_specs += [pl.BlockSpec((tm * SUBLANE, LANE), lambda i: (i, 0)), row(LANE), row(LANE)]
    return pl.pallas_call(
        functools.partial(_outproj_kernel, moe=moe),
        grid=(T // tm,),
        in_specs=in_specs,
        out_specs=tuple(out_specs),
        out_shape=tuple(out_shape),
        compiler_params=_params(("parallel",)),
        name="outproj_moe" if moe else "outproj",
    )(*ins)


def _ffn_kernel(y_ref, h_ref, wg_ref, wu_ref, wd_ref, out_ref, *, fc):
    y = y_ref[...]
    out_ref[...] = h_ref[...]
    for c in range(wg_ref.shape[1] // fc):
        cs = slice(c * fc, (c + 1) * fc)
        g = _dot(y, wg_ref[:, cs])
        u = _dot(y, wu_ref[:, cs])
        a = (g * _sigmoid(g) * u).astype(BF16)
        out_ref[...] += _dot(a, wd_ref[cs, :])


def _ffn(y, h, wg, wu, wd, tm, fc):
    T = h.shape[0]
    row = lambda n: pl.BlockSpec((tm, n), lambda i: (i, 0))
    const = lambda a: pl.BlockSpec(a.shape, lambda i: (0, 0))
    return pl.pallas_call(
        functools.partial(_ffn_kernel, fc=fc),
        grid=(T // tm,),
        in_specs=[row(1024), row(1024), const(wg), const(wu), const(wd)],
        out_specs=row(1024),
        out_shape=jax.ShapeDtypeStruct((T, D_MODEL), F32),
        compiler_params=_params(("parallel",)),
        name="ffn",
    )(y, h, wg, wu, wd)


def _row_copy(src, dst, sem, s, d):
    return pltpu.make_async_copy(
        src.at[pl.ds(pl.multiple_of(s * SUBLANE, SUBLANE), SUBLANE), :],
        dst.at[pl.ds(pl.multiple_of(d * SUBLANE, SUBLANE), SUBLANE), :], sem)


def _to_row_tiles(ref, x):
    n = x.shape[0]
    for s in range(D_MODEL // LANE):
        ref[pl.ds(s, n, stride=SUBLANE), :] = x[:, s * LANE:(s + 1) * LANE]


def _from_row_tiles(ref, n, s):
    return ref[pl.ds(s, n, stride=SUBLANE), :]


def _dispatch_kernel(s0_ref, s1_ref, y_ref, zero_hbm, x_hbm, sem, *, rows):
    del zero_hbm

    def issue(k, c):
        base = pl.multiple_of(k * SUBLANE, SUBLANE)
        for u in range(SUBLANE):
            _row_copy(y_ref, x_hbm, sem.at[0], base + u, s0_ref[0, 0, base + u]).start()
            _row_copy(y_ref, x_hbm, sem.at[1], base + u, s1_ref[0, 0, base + u]).start()
        return c

    lax.fori_loop(0, rows // SUBLANE, issue, 0)

    def drain(r, c):
        _row_copy(y_ref, x_hbm, sem.at[0], 0, 0).wait()
        _row_copy(y_ref, x_hbm, sem.at[1], 0, 0).wait()
        return c

    lax.fori_loop(0, rows, drain, 0, unroll=8)


def _dispatch(y_tiles, slot0, slot1, n_slots, rows):
    T = slot0.shape[0]
    n = T // rows
    smem = lambda: pl.BlockSpec((1, 1, rows), lambda i: (i, 0, 0), memory_space=pltpu.SMEM)
    zeros = jnp.zeros((n_slots * SUBLANE, LANE), F32)
    return pl.pallas_call(
        functools.partial(_dispatch_kernel, rows=rows),
        grid=(n,),
        in_specs=[smem(), smem(), pl.BlockSpec((rows * SUBLANE, LANE), lambda i: (i, 0)),
                  pl.BlockSpec(memory_space=pl.ANY)],
        out_specs=pl.BlockSpec(memory_space=pl.ANY),
        out_shape=jax.ShapeDtypeStruct((n_slots * SUBLANE, LANE), F32),
        input_output_aliases={3: 0},
        scratch_shapes=[pltpu.SemaphoreType.DMA((2,))],
        compiler_params=_params(("arbitrary",)),
        name="moe_dispatch",
    )(slot0.reshape(n, 1, rows), slot1.reshape(n, 1, rows), y_tiles, zeros)


def _expert_kernel(te_ref, tr_ref, x_ref, wg_ref, wu_ref, wd_ref, out_ref, xb, acc, *, tm):
    i = pl.program_id(0)
    f = pl.program_id(1)

    @pl.when(f == 0)
    def _():
        for s in range(D_MODEL // LANE):
            xb[:, s * LANE:(s + 1) * LANE] = _from_row_tiles(x_ref, tm, s).astype(BF16)
        acc[...] = jnp.zeros(acc.shape, F32)

    def swiglu(nrows):
        x = xb[0:nrows, :]
        g = _dot(x, wg_ref[...].astype(BF16))
        u = _dot(x, wu_ref[...].astype(BF16))
        a = (g * _sigmoid(g) * u).astype(BF16)
        acc[0:nrows, :] += _dot(a, wd_ref[...].astype(BF16))

    valid = tr_ref[i]

    @pl.when(valid > tm // 2)
    def _():
        swiglu(tm)

    @pl.when((valid > 0) & (valid <= tm // 2))
    def _():
        swiglu(tm // 2)

    @pl.when(f == pl.num_programs(1) - 1)
    def _():
        _to_row_tiles(out_ref, acc[...])


def _experts(x_sorted, tile_expert, tile_rows, wg, wu, wd, layer, tm, tf):
    P = x_sorted.shape[0] // SUBLANE
    n = P // tm
    F = wg.shape[3]
    grid_spec = pltpu.PrefetchScalarGridSpec(
        num_scalar_prefetch=2,
        grid=(n, F // tf),
        in_specs=[
            pl.BlockSpec((tm * SUBLANE, LANE), lambda i, f, te, tr: (i, 0)),
            pl.BlockSpec((None, None, D_MODEL, tf), lambda i, f, te, tr: (layer, te[i], 0, f)),
            pl.BlockSpec((None, None, D_MODEL, tf), lambda i, f, te, tr: (layer, te[i], 0, f)),
            pl.BlockSpec((None, None, tf, D_MODEL), lambda i, f, te, tr: (layer, te[i], f, 0)),
        ],
        out_specs=pl.BlockSpec((tm * SUBLANE, LANE), lambda i, f, te, tr: (i, 0)),
        scratch_shapes=[pltpu.VMEM((tm, D_MODEL), BF16), pltpu.VMEM((tm, D_MODEL), F32)],
    )
    return pl.pallas_call(
        functools.partial(_expert_kernel, tm=tm),
        grid_spec=grid_spec,
        out_shape=jax.ShapeDtypeStruct((P * SUBLANE, LANE), F32),
        compiler_params=_params(("parallel", "arbitrary")),
        name="moe_experts",
    )(tile_expert, tile_rows, x_sorted, wg, wu, wd)


def _combine_kernel(s0_ref, s1_ref, e_hbm, h_ref, tw_ref, *rest, rows, final):
    if final:
        gfin_ref, out_ref, b0, b1, sem = rest
    else:
        out_ref, b0, b1, sem = rest

    def issue(r, c):
        _row_copy(e_hbm, b0, sem.at[0], s0_ref[0, 0, r], r).start()
        _row_copy(e_hbm, b1, sem.at[1], s1_ref[0, 0, r], r).start()
        return c

    lax.fori_loop(0, rows, issue, 0, unroll=8)

    def drain(r, c):
        _row_copy(e_hbm, b0, sem.at[0], 0, r).wait()
        _row_copy(e_hbm, b1, sem.at[1], 0, r).wait()
        return c

    lax.fori_loop(0, rows, drain, 0, unroll=8)
    tw = tw_ref[...]
    for s in range(D_MODEL // LANE):
        cs = slice(s * LANE, (s + 1) * LANE)
        out_ref[:, cs] = (h_ref[:, cs] + tw[:, 0:1] * _from_row_tiles(b0, rows, s)
                          + tw[:, 1:2] * _from_row_tiles(b1, rows, s))
    if final:
        x = out_ref[...]
        ms = jnp.mean(x * x, axis=-1, keepdims=True)
        out_ref[...] = x * lax.rsqrt(ms + EPS) * gfin_ref[...]


def _combine(e_sorted, slot0, slot1, h, topw, rows, g_final=None):
    T = h.shape[0]
    n = T // rows
    final = g_final is not None
    smem = lambda: pl.BlockSpec((1, 1, rows), lambda i: (i, 0, 0), memory_space=pltpu.SMEM)
    ins = [slot0.reshape(n, 1, rows), slot1.reshape(n, 1, rows), e_sorted, h, topw]
    in_specs = [smem(), smem(), pl.BlockSpec(memory_space=pl.ANY),
                pl.BlockSpec((rows, D_MODEL), lambda i: (i, 0)),
                pl.BlockSpec((rows, LANE), lambda i: (i, 0))]
    if final:
        ins.append(g_final)
        in_specs.append(pl.BlockSpec((1, D_MODEL), lambda i: (0, 0)))
    return pl.pallas_call(
        functools.partial(_combine_kernel, rows=rows, final=final),
        grid=(n,),
        in_specs=in_specs,
        out_specs=pl.BlockSpec((rows, D_MODEL), lambda i: (i, 0)),
        out_shape=jax.ShapeDtypeStruct((T, D_MODEL), F32),
        scratch_shapes=[pltpu.VMEM((rows * SUBLANE, LANE), F32),
                        pltpu.VMEM((rows * SUBLANE, LANE), F32),
                        pltpu.SemaphoreType.DMA((2,))],
        compiler_params=_params(("arbitrary",)),
        name="moe_combine_final" if final else "moe_combine",
    )(*ins)


def _route(topi, tm):
    T = topi.shape[0]
    e = topi[:, :2].reshape(-1)
    onehot = (e[:, None] == jnp.arange(N_EXPERTS, dtype=jnp.int32)[None, :]).astype(jnp.int32)
    csum = jnp.cumsum(onehot, axis=0)
    pos = jnp.sum((csum - onehot) * onehot, axis=1)
    counts = csum[-1]
    tiles = (counts + tm - 1) // tm
    tile_end = jnp.cumsum(tiles)
    offs = (tile_end - tiles) * tm
    slot = jnp.sum(onehot * offs[None, :], axis=1) + pos
    n_tiles = 2 * T // tm + N_EXPERTS
    tile_ids = jnp.arange(n_tiles, dtype=jnp.int32)
    tile_expert = jnp.sum((tile_ids[:, None] >= tile_end[None, :]).astype(jnp.int32), axis=1)
    tile_expert = jnp.minimum(tile_expert, N_EXPERTS - 1).astype(jnp.int32)
    first_tile = (tile_end - tiles)[tile_expert]
    tile_rows = jnp.clip(counts[tile_expert] - (tile_ids - first_tile) * tm, 0, tm)
    tile_rows = jnp.where(tile_ids < tile_end[-1], tile_rows, 0).astype(jnp.int32)
    slot2 = slot.reshape(T, 2)
    return n_tiles * tm, tile_expert, tile_rows, slot2[:, 0], slot2[:, 1]


def _norm_kernel(x_ref, g_ref, o_ref):
    x = x_ref[...]
    ms = jnp.mean(x * x, axis=-1, keepdims=True)
    o_ref[...] = x * lax.rsqrt(ms + EPS) * g_ref[...]


def _final_norm(h, g, tm):
    T = h.shape[0]
    return pl.pallas_call(
        _norm_kernel,
        grid=(T // tm,),
        in_specs=[pl.BlockSpec((tm, D_MODEL), lambda i: (i, 0)),
                  pl.BlockSpec((1, D_MODEL), lambda i: (0, 0))],
        out_specs=pl.BlockSpec((tm, D_MODEL), lambda i: (i, 0)),
        out_shape=jax.ShapeDtypeStruct((T, D_MODEL), F32),
        compiler_params=_params(("parallel",)),
        name="final_norm",
    )(h, g)


def _pad_cols(w, n):
    return jnp.pad(w, ((0, 0), (0, n - w.shape[1])))


def _prep_w_in(w):
    gates0 = 4 * D_MLSTM
    gi = _pad_cols(w[:, gates0:gates0 + H_MLSTM], LANE)
    gf = _pad_cols(w[:, gates0 + H_MLSTM:gates0 + 2 * H_MLSTM], LANE)
    rest = w[:, gates0 + 2 * H_MLSTM:]
    return jnp.concatenate([w[:, :gates0], rest, gi, gf], axis=1).astype(BF16)


def kernel(x, norm_mix_g, w_in, b_igate, b_fgate, conv_w, conv_b, g_mlstm, g_sb, w_out,
           norm_ffn_g, ffn_w_gate, ffn_w_up, ffn_w_down, w_router, moe_w_gate, moe_w_up,
           moe_w_down, norm_final_g):
    B, S, D = x.shape
    T = B * S
    depth = w_in.shape[0]
    tm = min(512, T)
    ts = min(256, S)
    tq = min(256, S)
    tm_moe = min(1024, T)
    rows = min(1024, T)

    ii = jnp.arange(CHUNK)
    tri = (ii[None, :] <= ii[:, None]).astype(BF16)
    jj = jnp.arange(tq)
    upper = (jj[:, None] > jj[None, :]).astype(BF16)
    usuf = jnp.concatenate([upper, jnp.ones((tq, LANE), BF16)], axis=1)

    f_dense = ffn_w_gate.shape[2]
    f_pad = -(-f_dense // 256) * 256

    h = x.reshape(T, D)
    for layer in range(depth):
        w1 = _prep_w_in(w_in[layer])
        qk, v_m, o_m, q_s, k_s, v_s, gi, gf = _inproj(h, norm_mix_g[layer][None, :], w1, tm)
        bi = _pad_cols(b_igate[layer][None, :], LANE)
        bf = _pad_cols(b_fgate[layer][None, :], LANE)
        h_m = _mlstm(qk, v_m, o_m, gi, gf, conv_w[layer], conv_b[layer][None, :], bi, bf,
                     g_mlstm[layer][None, :], tri, B, S, ts)
        h_s = _sb_attention(q_s, k_s, v_s, g_sb[layer][None, :], usuf, B, S, tq, 4)
        wo = w_out[layer].astype(BF16)
        gffn = norm_ffn_g[layer][None, :]
        j = layer // 2
        if layer % 2 == 0:
            h, yb = _outproj(h_m, h_s, wo, h, gffn, tm)
            wg = _pad_cols(ffn_w_gate[j], f_pad).astype(BF16)
            wu = _pad_cols(ffn_w_up[j], f_pad).astype(BF16)
            wd = jnp.pad(ffn_w_down[j], ((0, f_pad - f_dense), (0, 0))).astype(BF16)
            h = _ffn(yb, h, wg, wu, wd, tm, 256)
        else:
            wr = _pad_cols(w_router[j], LANE)
            wrh = wr.astype(BF16)
            wrl = (wr - wrh.astype(F32)).astype(BF16)
            h, yb, yf, topi, topw = _outproj(h_m, h_s, wo, h, gffn, tm, router=(wrh, wrl))
            n_slots, tile_expert, tile_rows, slot0, slot1 = _route(topi, tm_moe)
            x_sorted = _dispatch(yf, slot0, slot1, n_slots, rows)
            e_sorted = _experts(x_sorted, tile_expert, tile_rows, moe_w_gate, moe_w_up,
                                moe_w_down, j, tm_moe, 512)
            last = layer == depth - 1
            h = _combine(e_sorted, slot0, slot1, h, topw, rows,
                         g_final=norm_final_g[None, :] if last else None)
    out = h if depth % 2 == 0 else _final_norm(h, norm_final_g[None, :], tm)
    return out.reshape(B, S, D)
```
